```python
import math
import jax, jax.numpy as jnp
from jax import lax
import numpy as np

D_MODEL = 1024
BATCH = 8
SEQ = 4096
DEPTH = 4

N_MIXERS = 2
N_RET_LAYERS = (DEPTH + N_MIXERS - 1) // N_MIXERS
N_NSA_LAYERS = DEPTH // N_MIXERS
NORM_EPS = 1e-6
NEG = -1e30

RET_HEADS = 4
RET_QK_DIM = D_MODEL // RET_HEADS
RET_V_DIM = 2 * D_MODEL // RET_HEADS
RET_CHUNK = 128
RET_ROPE_THETA = 10000.0
RET_IN = 2 * D_MODEL + 2 * (2 * D_MODEL)

NSA_HEADS = 16
NSA_KV_HEADS = 4
NSA_GROUP = NSA_HEADS // NSA_KV_HEADS
NSA_HEAD_DIM = D_MODEL // NSA_HEADS
ROPE_THETA = 500000.0
ROPE_DIM = NSA_HEAD_DIM // 4
CMP_BLOCK = 32
CMP_STRIDE = 16
CMP_HIDDEN = 4 * NSA_HEAD_DIM
SLC_BLOCK = 64
SLC_TOPK = 16
WINDOW = 512
NSA_Q_BLOCK = 32
N_BRANCH = 3
FORCE = 1e4
NSA_QD = NSA_HEADS * NSA_HEAD_DIM
NSA_KVD = NSA_KV_HEADS * NSA_HEAD_DIM
NSA_IN = NSA_QD + N_BRANCH * 2 * NSA_KVD + NSA_HEADS * N_BRANCH

FFN_HIDDEN = ((8 * D_MODEL // 3 + 255) // 256) * 256

kernel_name = "hybrid_retention_nsa_swiglu"


def rms_norm(x, gain):
    xf = x.astype(jnp.float32)
    y = xf * lax.rsqrt(jnp.mean(xf * xf, axis=-1, keepdims=True) + NORM_EPS)
    return (y * gain.astype(jnp.float32)).astype(x.dtype)


def apply_rotary(x, positions, rot_dim, theta):
    half = rot_dim // 2
    inv_freq = theta ** (-2.0 * jnp.arange(half, dtype=jnp.float32) / rot_dim)
    ang = positions.astype(jnp.float32)[..., None] * inv_freq
    cos = jnp.cos(ang)[:, :, None, :]
    sin = jnp.sin(ang)[:, :, None, :]
    xr = x[..., :rot_dim].astype(jnp.float32)
    x1, x2 = xr[..., :half], xr[..., half:]
    rot = jnp.concatenate([x1 * cos - x2 * sin, x2 * cos + x1 * sin], axis=-1).astype(x.dtype)
    return jnp.concatenate([rot, x[..., rot_dim:]], axis=-1)


def chunkwise_retention(q, k, v):
    B, T, H, dk = q.shape
    dv = v.shape[-1]
    C = RET_CHUNK
    N = T // C
    log_g = jnp.log(1.0 - 2.0 ** (-5.0 - jnp.arange(H, dtype=jnp.float32)))
    idx = jnp.arange(C, dtype=jnp.float32)
    diff = idx[:, None] - idx[None, :]
    intra = jnp.where(diff >= 0, jnp.exp(log_g[:, None, None] * jnp.maximum(diff, 0.0)), 0.0)
    q_decay = jnp.exp(log_g[:, None] * (idx + 1.0))[None, :, :, None]
    k_decay = jnp.exp(log_g[:, None] * (C - 1.0 - idx))[None, :, :, None]
    chunk_decay = jnp.exp(log_g * C)[None, :, None, None]

    def to_chunks(a):
        return a.astype(jnp.float32).reshape(B, N, C, H, a.shape[-1]).transpose(1, 0, 3, 2, 4)

    def step(state, inp):
        qc, kc, vc = inp
        s = jnp.einsum('bhid,bhjd->bhij', qc, kc) * intra
        o = (jnp.einsum('bhij,bhje->bhie', s, vc)
             + jnp.einsum('bhid,bhde->bhie', qc * q_decay, state))
        state = state * chunk_decay + jnp.einsum('bhjd,bhje->bhde', kc * k_decay, vc)
        return state, o

    state0 = jnp.zeros((B, H, dk, dv), jnp.float32)
    _, o = lax.scan(step, state0, (to_chunks(q), to_chunks(k), to_chunks(v)))
    return o.transpose(1, 0, 3, 2, 4).reshape(B, T, H, dv)


def retention_mixer(h, positions, w_in, w_out):
    B, T, _ = h.shape
    proj = h @ w_in
    q, k, v, g = jnp.split(proj, [D_MODEL, 2 * D_MODEL, 4 * D_MODEL], axis=-1)
    q = q.reshape(B, T, RET_HEADS, RET_QK_DIM)
    k = k.reshape(B, T, RET_HEADS, RET_QK_DIM) * (RET_QK_DIM ** -0.5)
    v = v.reshape(B, T, RET_HEADS, RET_V_DIM)
    q = apply_rotary(q, positions, RET_QK_DIM, RET_ROPE_THETA)
    k = apply_rotary(k, positions, RET_QK_DIM, RET_ROPE_THETA)
    o = chunkwise_retention(q, k, v)
    o = o * lax.rsqrt(jnp.mean(o * o, axis=-1, keepdims=True) + NORM_EPS)
    o = o.astype(h.dtype).reshape(B, T, 2 * D_MODEL) * jax.nn.silu(g)
    return o @ w_out


def nsa_mixer(h, positions, w_in, cmp_pos, cmp_w1, cmp_w2, w_out):
    B, T, _ = h.shape
    G, HG, HD = NSA_KV_HEADS, NSA_GROUP, NSA_HEAD_DIM
    scale = HD ** -0.5
    proj = h @ w_in
    q = proj[..., :NSA_QD].reshape(B, T, NSA_HEADS, HD)
    kv = proj[..., NSA_QD:NSA_QD + N_BRANCH * 2 * NSA_KVD].reshape(B, T, N_BRANCH, 2, G, HD)
    gates = jax.nn.sigmoid(proj[..., NSA_QD + N_BRANCH * 2 * NSA_KVD:].astype(jnp.float32))
    gates = gates.astype(h.dtype).reshape(B, T, NSA_HEADS, N_BRANCH)

    q = apply_rotary(q, positions, ROPE_DIM, ROPE_THETA)
    k_cmp_tok = apply_rotary(kv[:, :, 0, 0], positions, ROPE_DIM, ROPE_THETA)
    v_cmp_tok = kv[:, :, 0, 1]
    k_slc = apply_rotary(kv[:, :, 1, 0], positions, ROPE_DIM, ROPE_THETA)
    v_slc = kv[:, :, 1, 1]
    k_win = apply_rotary(kv[:, :, 2, 0], positions, ROPE_DIM, ROPE_THETA)
    v_win = kv[:, :, 2, 1]

    n_cmp = (T - CMP_BLOCK) // CMP_STRIDE + 1
    tok_idx = jnp.arange(n_cmp)[:, None] * CMP_STRIDE + jnp.arange(CMP_BLOCK)[None, :]

    def compress(a, pos_emb, w1, w2):
        blocks = a[:, tok_idx] + pos_emb[None, None, :, None, :]
        blocks = blocks.transpose(0, 1, 3, 2, 4).reshape(B, n_cmp, G, CMP_BLOCK * HD)
        return jax.nn.silu(blocks @ w1) @ w2

    k_cmp = compress(k_cmp_tok, cmp_pos[0], cmp_w1[0], cmp_w2[0])
    v_cmp = compress(v_cmp_tok, cmp_pos[1], cmp_w1[1], cmp_w2[1])
    cmp_end = jnp.arange(n_cmp) * CMP_STRIDE + CMP_BLOCK - 1

    NB = T // SLC_BLOCK
    n_sel = min(SLC_TOPK, NB)
    cmp_start = jnp.arange(n_cmp) * CMP_STRIDE
    slc_start = jnp.arange(NB) * SLC_BLOCK
    overlap = ((cmp_start[:, None] < slc_start[None, :] + SLC_BLOCK)
               & (cmp_start[:, None] + CMP_BLOCK > slc_start[None, :])).astype(jnp.float32)
    k_blocks = k_slc.reshape(B, NB, SLC_BLOCK, G, HD).transpose(0, 3, 1, 2, 4)
    v_blocks = v_slc.reshape(B, NB, SLC_BLOCK, G, HD).transpose(0, 3, 1, 2, 4)
    b_ix = jnp.arange(B)[:, None, None, None]
    g_ix = jnp.arange(G)[None, :, None, None]
    blk = jnp.arange(NB)

    k_win_pad = jnp.pad(k_win, ((0, 0), (WINDOW, 0), (0, 0), (0, 0)))
    v_win_pad = jnp.pad(v_win, ((0, 0), (WINDOW, 0), (0, 0), (0, 0)))
    WL = WINDOW + NSA_Q_BLOCK

    def attend(scores, mask, vals, eq_out):
        p = jax.nn.softmax(jnp.where(mask, scores, NEG), axis=-1)
        return p, jnp.einsum(eq_out, p.astype(vals.dtype), vals, preferred_element_type=jnp.float32)

    def q_block(n):
        q0 = n * NSA_Q_BLOCK
        qb = lax.dynamic_slice_in_dim(q, q0, NSA_Q_BLOCK, axis=1).reshape(B, NSA_Q_BLOCK, G, HG, HD)
        gb = lax.dynamic_slice_in_dim(gates, q0, NSA_Q_BLOCK, axis=1).reshape(B, NSA_Q_BLOCK, G, HG, N_BRANCH)
        t_pos = q0 + jnp.arange(NSA_Q_BLOCK)

        s_c = jnp.einsum('bqghd,bkgd->bghqk', qb, k_cmp, preferred_element_type=jnp.float32) * scale
        valid_c = cmp_end[None, :] <= t_pos[:, None]
        p_c, o_cmp = attend(s_c, valid_c, v_cmp, 'bghqk,bkgd->bqghd')
        p_c = p_c * valid_c
        o_cmp = o_cmp * jnp.any(valid_c, axis=-1)[None, :, None, None, None]

        imp = jnp.einsum('bghqk,kn->bgqn', p_c, overlap)
        cur = t_pos // SLC_BLOCK
        forced = (blk[None, :] == 0) | (blk[None, :] == cur[:, None]) | (blk[None, :] == cur[:, None] - 1)
        future = blk[None, :] > cur[:, None]
        imp = jnp.where(future, NEG, jnp.where(forced, imp + FORCE, imp))
        _, sel = lax.top_k(imp, n_sel)
        ks = k_blocks[b_ix, g_ix, sel].reshape(B, G, NSA_Q_BLOCK, n_sel * SLC_BLOCK, HD)
        vs = v_blocks[b_ix, g_ix, sel].reshape(B, G, NSA_Q_BLOCK, n_sel * SLC_BLOCK, HD)
        tok = (sel[..., None] * SLC_BLOCK + jnp.arange(SLC_BLOCK)).reshape(B, G, NSA_Q_BLOCK, n_sel * SLC_BLOCK)
        tmask = (tok <= t_pos[None, None, :, None])[:, :, None]
        s_s = jnp.einsum('bqghd,bgqkd->bghqk', qb, ks, preferred_element_type=jnp.float32) * scale
        _, o_slc = attend(s_s, tmask, vs, 'bghqk,bgqkd->bqghd')

        kw = lax.dynamic_slice_in_dim(k_win_pad, q0, WL, axis=1)
        vw = lax.dynamic_slice_in_dim(v_win_pad, q0, WL, axis=1)
        kpos = q0 - WINDOW + jnp.arange(WL)
        wmask = ((kpos[None, :] <= t_pos[:, None]) & (kpos[None, :] > t_pos[:, None] - WINDOW)
                 & (kpos[None, :] >= 0))
        s_w = jnp.einsum('bqghd,bkgd->bghqk', qb, kw, preferred_element_type=jnp.float32) * scale
        _, o_win = attend(s_w, wmask, vw, 'bghqk,bkgd->bqghd')

        gf = gb.astype(jnp.float32)
        o = gf[..., 0:1] * o_cmp + gf[..., 1:2] * o_slc + gf[..., 2:3] * o_win
        return o.astype(h.dtype).reshape(B, NSA_Q_BLOCK, NSA_QD)

    out = lax.map(q_block, jnp.arange(T // NSA_Q_BLOCK))
    out = out.transpose(1, 0, 2, 3).reshape(B, T, NSA_QD)
    return out @ w_out


def swiglu(h, w_gu, w_down):
    a, b = jnp.split(h @ w_gu, 2, axis=-1)
    return (jax.nn.silu(a) * b) @ w_down


def setup_inputs(seed: int = 0) -> dict:
    key = jax.random.key(seed)
    ks = jax.random.split(key, 16)
    f32 = jnp.float32

    def nrm(k, shape, fan_in):
        return jax.random.normal(k, shape, f32) * (fan_in ** -0.5)

    x = jax.random.normal(ks[0], (BATCH, SEQ, D_MODEL), f32)
    offs = jax.random.randint(ks[1], (BATCH, 1), 0, 1024, dtype=jnp.int32)
    positions = offs + jnp.arange(SEQ, dtype=jnp.int32)[None, :]
    norm_mix = 1.0 + 0.01 * jax.random.normal(ks[2], (DEPTH, D_MODEL), f32)
    norm_ffn = 1.0 + 0.01 * jax.random.normal(ks[3], (DEPTH, D_MODEL), f32)
    norm_final = 1.0 + 0.01 * jax.random.normal(ks[4], (D_MODEL,), f32)
    ret_w_in = nrm(ks[5], (N_RET_LAYERS, D_MODEL, RET_IN), D_MODEL)
    ret_w_out = nrm(ks[6], (N_RET_LAYERS, 2 * D_MODEL, D_MODEL), 2 * D_MODEL)
    nsa_w_in = nrm(ks[7], (N_NSA_LAYERS, D_MODEL, NSA_IN), D_MODEL)
    nsa_cmp_pos = 0.1 * jax.random.normal(ks[8], (N_NSA_LAYERS, 2, CMP_BLOCK, NSA_HEAD_DIM), f32)
    nsa_cmp_w1 = nrm(ks[9], (N_NSA_LAYERS, 2, CMP_BLOCK * NSA_HEAD_DIM, CMP_HIDDEN), CMP_BLOCK * NSA_HEAD_DIM)
    nsa_cmp_w2 = nrm(ks[10], (N_NSA_LAYERS, 2, CMP_HIDDEN, NSA_HEAD_DIM), CMP_HIDDEN)
    nsa_w_out = nrm(ks[11], (N_NSA_LAYERS, NSA_QD, D_MODEL), NSA_QD)
    ffn_w_gu = nrm(ks[12], (DEPTH, D_MODEL, 2 * FFN_HIDDEN), D_MODEL)
    ffn_w_down = nrm(ks[13], (DEPTH, FFN_HIDDEN, D_MODEL), FFN_HIDDEN)
    return {"x": x, "positions": positions, "norm_mix": norm_mix, "norm_ffn": norm_ffn,
            "norm_final": norm_final, "ret_w_in": ret_w_in, "ret_w_out": ret_w_out,
            "nsa_w_in": nsa_w_in, "nsa_cmp_pos": nsa_cmp_pos, "nsa_cmp_w1": nsa_cmp_w1,
            "nsa_cmp_w2": nsa_cmp_w2, "nsa_w_out": nsa_w_out, "ffn_w_gu": ffn_w_gu,
            "ffn_w_down": ffn_w_down}


def reference(x, positions, norm_mix, norm_ffn, norm_final, ret_w_in, ret_w_out, nsa_w_in,
              nsa_cmp_pos, nsa_cmp_w1, nsa_cmp_w2, nsa_w_out, ffn_w_gu, ffn_w_down):
    for i in range(DEPTH):
        hn = rms_norm(x, norm_mix[i])
        j = i // N_MIXERS
        if i % N_MIXERS == 0:
            x = x + retention_mixer(hn, positions, ret_w_in[j], ret_w_out[j])
        else:
            x = x + nsa_mixer(hn, positions, nsa_w_in[j], nsa_cmp_pos[j], nsa_cmp_w1[j],
                              nsa_cmp_w2[j], nsa_w_out[j])
        x = x + swiglu(rms_norm(x, norm_ffn[i]), ffn_w_gu[i], ffn_w_down[i])
    return rms_norm(x, norm_final)
```

```python
import functools

import numpy as np
import jax
import jax.numpy as jnp
from jax import lax
from jax.experimental import pallas as pl
from jax.experimental.pallas import tpu as pltpu

F32 = jnp.float32
BF16 = jnp.bfloat16

D_MODEL = 1024
DEPTH = 4
NORM_EPS = 1e-6
NEG = -1e30

RET_HEADS = 4
RET_QK_DIM = 256
RET_V_DIM = 512
RET_ROPE_THETA = 10000.0
RET_IN = 6144
RET_CHUNK = 256

NSA_HEADS = 16
NSA_KV_HEADS = 4
NSA_HEAD_DIM = 64
NSA_ROPE_THETA = 500000.0
NSA_ROPE_DIM = 16
CMP_BLOCK = 32
CMP_STRIDE = 16
CMP_HIDDEN = 256
SLC_BLOCK = 64
SLC_TOPK = 16
WINDOW = 512
FORCE = 1e4
NSA_QD = 1024
NSA_KVD = 256
FFN_HIDDEN = 2816

LANES = 128
MXU_DIM = 256


def _nt_dot(a, b):
    return lax.dot_general(a, b, (((1,), (1,)), ((), ())), preferred_element_type=F32)


def _dot(a, b):
    return jnp.dot(a, b, preferred_element_type=F32)


def _rms_to_bf16(x, gain):
    ms = jnp.mean(x * x, axis=-1, keepdims=True)
    return (x * lax.rsqrt(ms + NORM_EPS) * gain).astype(BF16)


def _rope_tables_kernel(pos_ref, c_ref, cos_ref, sa_ref, sb_ref):
    ang = pos_ref[...].astype(F32) * c_ref[0:1, :]
    s = jnp.sin(ang)
    cos_ref[...] = jnp.cos(ang)
    sa_ref[...] = s * c_ref[1:2, :]
    sb_ref[...] = s * c_ref[2:3, :]


def _rope_tables(pos_col, consts, tm):
    m = pos_col.shape[0]
    out = jax.ShapeDtypeStruct((m, LANES), F32)
    return pl.pallas_call(
        _rope_tables_kernel,
        grid=(m // tm,),
        in_specs=[pl.BlockSpec((tm, 1), lambda i: (i, 0)),
                  pl.BlockSpec((8, LANES), lambda i: (0, 0))],
        out_specs=[pl.BlockSpec((tm, LANES), lambda i: (i, 0))] * 3,
        out_shape=[out, out, out],
        name="rope_tables",
    )(pos_col, consts)


def _ret_proj_kernel(x_ref, gain_ref, w_ref, cos_ref, sin_ref, o_ref, hn_ref):
    j = pl.program_id(1)

    @pl.when(j == 0)
    def _():
        hn_ref[...] = _rms_to_bf16(x_ref[...], gain_ref[...])

    acc = _dot(hn_ref[...], w_ref[...])

    @pl.when(j < 8)
    def _():
        scale = jnp.where(j >= 4, RET_QK_DIM ** -0.5, 1.0).astype(F32)
        x1 = acc[:, :LANES]
        x2 = acc[:, LANES:]
        c = cos_ref[...]
        s = sin_ref[...]
        o_ref[:, :LANES] = ((x1 * c - x2 * s) * scale).astype(BF16)
        o_ref[:, LANES:] = ((x2 * c + x1 * s) * scale).astype(BF16)

    @pl.when((j >= 8) & (j < 16))
    def _():
        o_ref[...] = acc.astype(BF16)

    @pl.when(j >= 16)
    def _():
        o_ref[...] = (acc * jax.nn.sigmoid(acc)).astype(BF16)


def _ret_proj(x2d, gain, w_bf16, cos, sin, tm):
    m = x2d.shape[0]
    tn = MXU_DIM
    return pl.pallas_call(
        _ret_proj_kernel,
        grid=(m // tm, RET_IN // tn),
        in_specs=[pl.BlockSpec((tm, D_MODEL), lambda i, j: (i, 0)),
                  pl.BlockSpec((1, D_MODEL), lambda i, j: (0, 0)),
                  pl.BlockSpec((D_MODEL, tn), lambda i, j: (0, j)),
                  pl.BlockSpec((tm, LANES), lambda i, j: (i, 0)),
                  pl.BlockSpec((tm, LANES), lambda i, j: (i, 0))],
        out_specs=pl.BlockSpec((tm, tn), lambda i, j: (i, j)),
        out_shape=jax.ShapeDtypeStruct((m, RET_IN), BF16),
        scratch_shapes=[pltpu.VMEM((tm, D_MODEL), BF16)],
        compiler_params=pltpu.CompilerParams(dimension_semantics=("arbitrary", "arbitrary")),
        name="ret_proj",
    )(x2d, gain, w_bf16, cos, sin)


def _retention_kernel(q_ref, k_ref, v_ref, g_ref, o_ref, state, intra, qd, kd, *, chunk, n_chunks):
    c_len = chunk
    h = pl.program_id(1)
    t = pl.program_id(2)

    @pl.when(t == 0)
    def _():
        hv = jnp.full((c_len, RET_QK_DIM), h, jnp.int32)
        den = jnp.left_shift(jnp.full((c_len, RET_QK_DIM), 32, jnp.int32), hv).astype(F32)
        lg = jnp.log(1.0 - 1.0 / den)
        ri = lax.broadcasted_iota(jnp.int32, (c_len, RET_QK_DIM), 0)
        ci = lax.broadcasted_iota(jnp.int32, (c_len, RET_QK_DIM), 1)
        diff = (ri - ci).astype(F32)
        intra[...] = jnp.where(diff >= 0, jnp.exp(lg * jnp.maximum(diff, 0.0)), 0.0)
        rf = ri.astype(F32)
        qd[...] = jnp.exp(lg * (rf + 1.0))
        kd[...] = jnp.exp(lg * (c_len - 1.0 - rf))
        state[...] = jnp.zeros_like(state)

    chunk_decay = qd[c_len - 1:c_len, 0:1]
    for c in range(n_chunks):
        sl = pl.ds(c * c_len, c_len)
        qc = q_ref[0, sl, :]
        kc = k_ref[0, sl, :]
        vc = v_ref[0, sl, :]
        s = _nt_dot(qc, kc) * intra[...]
        st = state[...]
        qdec = (qc.astype(F32) * qd[...]).astype(BF16)
        o = _dot(s.astype(BF16), vc) + _dot(qdec, st.astype(BF16))
        kdec = (kc.astype(F32) * kd[...]).astype(BF16)
        state[...] = st * chunk_decay + lax.dot_general(
            kdec, vc, (((0,), (0,)), ((), ())), preferred_element_type=F32)
        ms = jnp.mean(o * o, axis=-1, keepdims=True)
        y = o * lax.rsqrt(ms + NORM_EPS)
        o_ref[0, sl, :] = (y * g_ref[0, sl, :].astype(F32)).astype(BF16)


def _retention(proj3, tq):
    b, t, _ = proj3.shape
    chunk = RET_CHUNK
    assert RET_CHUNK == RET_QK_DIM and tq % chunk == 0 and t % tq == 0
    kern = functools.partial(_retention_kernel, chunk=chunk, n_chunks=tq // chunk)
    nq = D_MODEL // RET_QK_DIM
    return pl.pallas_call(
        kern,
        grid=(b, RET_HEADS, t // tq),
        in_specs=[pl.BlockSpec((1, tq, RET_QK_DIM), lambda bb, h, tt: (bb, tt, h)),
                  pl.BlockSpec((1, tq, RET_QK_DIM), lambda bb, h, tt: (bb, tt, nq + h)),
                  pl.BlockSpec((1, tq, RET_V_DIM), lambda bb, h, tt: (bb, tt, nq + h)),
                  pl.BlockSpec((1, tq, RET_V_DIM), lambda bb, h, tt: (bb, tt, 2 * nq + h))],
        out_specs=pl.BlockSpec((1, tq, RET_V_DIM), lambda bb, h, tt: (bb, tt, h)),
        out_shape=jax.ShapeDtypeStruct((b, t, RET_HEADS * RET_V_DIM), BF16),
        scratch_shapes=[pltpu.VMEM((RET_QK_DIM, RET_V_DIM), F32),
                        pltpu.VMEM((chunk, RET_QK_DIM), F32),
                        pltpu.VMEM((chunk, RET_QK_DIM), F32),
                        pltpu.VMEM((chunk, RET_QK_DIM), F32)],
        compiler_params=pltpu.CompilerParams(dimension_semantics=("arbitrary",) * 3),
        name="retention",
    )(proj3, proj3, proj3, proj3)


def _matmul_res_kernel(a_ref, w_ref, res_ref, *rest, final_norm):
    if final_norm:
        gain_ref, o_ref = rest
    else:
        (o_ref,) = rest
    y = res_ref[...] + _dot(a_ref[...], w_ref[...])
    if final_norm:
        ms = jnp.mean(y * y, axis=-1, keepdims=True)
        y = y * lax.rsqrt(ms + NORM_EPS) * gain_ref[...]
    o_ref[...] = y


def _matmul_res(a, w_bf16, res, tm, final_gain=None):
    m, k = a.shape
    n = w_bf16.shape[1]
    in_specs = [pl.BlockSpec((tm, k), lambda i: (i, 0)),
                pl.BlockSpec((k, n), lambda i: (0, 0)),
                pl.BlockSpec((tm, n), lambda i: (i, 0))]
    args = [a, w_bf16, res]
    if final_gain is not None:
        in_specs.append(pl.BlockSpec((1, n), lambda i: (0, 0)))
        args.append(final_gain)
    return pl.pallas_call(
        functools.partial(_matmul_res_kernel, final_norm=final_gain is not None),
        grid=(m // tm,),
        in_specs=in_specs,
        out_specs=pl.BlockSpec((tm, n), lambda i: (i, 0)),
        out_shape=jax.ShapeDtypeStruct((m, n), F32),
        compiler_params=pltpu.CompilerParams(dimension_semantics=("arbitrary",)),
        name="matmul_res",
    )(*args)


def _ffn_gu_kernel(x_ref, gain_ref, wg_ref, wu_ref, o_ref, hn_ref):
    @pl.when(pl.program_id(1) == 0)
    def _():
        hn_ref[...] = _rms_to_bf16(x_ref[...], gain_ref[...])

    hn = hn_ref[...]
    a = _dot(hn, wg_ref[...])
    b = _dot(hn, wu_ref[...])
    o_ref[...] = (a * jax.nn.sigmoid(a) * b).astype(BF16)


def _ffn_gu(x2d, gain, w_gu_bf16, tm):
    m = x2d.shape[0]
    tn = MXU_DIM
    nj = FFN_HIDDEN // tn
    return pl.pallas_call(
        _ffn_gu_kernel,
        grid=(m // tm, nj),
        in_specs=[pl.BlockSpec((tm, D_MODEL), lambda i, j: (i, 0)),
                  pl.BlockSpec((1, D_MODEL), lambda i, j: (0, 0)),
                  pl.BlockSpec((D_MODEL, tn), lambda i, j: (0, j)),
                  pl.BlockSpec((D_MODEL, tn), lambda i, j: (0, j + nj))],
        out_specs=pl.BlockSpec((tm, tn), lambda i, j: (i, j)),
        out_shape=jax.ShapeDtypeStruct((m, FFN_HIDDEN), BF16),
        scratch_shapes=[pltpu.VMEM((tm, D_MODEL), BF16)],
        compiler_params=pltpu.CompilerParams(dimension_semantics=("arbitrary", "arbitrary")),
        name="ffn_gate_up",
    )(x2d, gain, w_gu_bf16, w_gu_bf16)


NSA_MAIN = NSA_QD + 6 * NSA_KVD
NSA_GATE_PAD = NSA_KV_HEADS * LANES
N_MAIN_BLOCKS = NSA_MAIN // MXU_DIM


def _partial_rotary(a, c, sa, sb):
    return a * c + pltpu.roll(a, 8, 1) * sa + pltpu.roll(a, LANES - 8, 1) * sb


def _nsa_proj_kernel(x_ref, gain_ref, w_ref, c_ref, sa_ref, sb_ref, o_ref, gate_ref, hn_ref):
    j = pl.program_id(1)

    @pl.when(j == 0)
    def _():
        hn_ref[...] = _rms_to_bf16(x_ref[...], gain_ref[...])

    acc = _dot(hn_ref[...], w_ref[...])

    @pl.when(j < 7)
    def _():
        scale = jnp.where(j < 4, NSA_HEAD_DIM ** -0.5, 1.0).astype(F32)
        c = c_ref[...]
        sa = sa_ref[...]
        sb = sb_ref[...]
        for half in range(2):
            sl = slice(half * LANES, (half + 1) * LANES)
            o_ref[:, sl] = (_partial_rotary(acc[:, sl], c, sa, sb) * scale).astype(BF16)

    @pl.when((j >= 7) & (j < N_MAIN_BLOCKS))
    def _():
        o_ref[...] = acc.astype(BF16)

    @pl.when(j >= N_MAIN_BLOCKS)
    def _():
        gate_ref[...] = jax.nn.sigmoid(acc)


def _nsa_proj(x2d, gain, w_bf16, c, sa, sb, tm):
    m = x2d.shape[0]
    tn = MXU_DIM
    nj = (NSA_MAIN + NSA_GATE_PAD) // tn
    last = N_MAIN_BLOCKS - 1
    return pl.pallas_call(
        _nsa_proj_kernel,
        grid=(m // tm, nj),
        in_specs=[pl.BlockSpec((tm, D_MODEL), lambda i, j: (i, 0)),
                  pl.BlockSpec((1, D_MODEL), lambda i, j: (0, 0)),
                  pl.BlockSpec((D_MODEL, tn), lambda i, j: (0, j)),
                  pl.BlockSpec((tm, LANES), lambda i, j: (i, 0)),
                  pl.BlockSpec((tm, LANES), lambda i, j: (i, 0)),
                  pl.BlockSpec((tm, LANES), lambda i, j: (i, 0))],
        out_specs=[pl.BlockSpec((tm, tn), lambda i, j: (i, jnp.minimum(j, last))),
                   pl.BlockSpec((tm, tn), lambda i, j: (i, jnp.maximum(j - N_MAIN_BLOCKS, 0)))],
        out_shape=[jax.ShapeDtypeStruct((m, NSA_MAIN), BF16),
                   jax.ShapeDtypeStruct((m, NSA_GATE_PAD), F32)],
        scratch_shapes=[pltpu.VMEM((tm, D_MODEL), BF16)],
        compiler_params=pltpu.CompilerParams(dimension_semantics=("arbitrary", "arbitrary")),
        name="nsa_proj",
    )(x2d, gain, w_bf16, c, sa, sb)


def _compress_kernel(a_ref, pos_ref, w1_ref, w2_ref, o_ref):
    a = a_ref[0, 0, 0]
    nc = a.shape[0]
    half = CMP_STRIDE * NSA_HEAD_DIM
    upper = _dot(a, w1_ref[0, :half, :])
    lower = _dot(a, w1_ref[0, half:, :])
    cpos = _dot(pos_ref[0], w1_ref[0])
    hid = upper + pltpu.roll(lower, nc - 1, 0) + cpos[0:1, :]
    hid = hid * jax.nn.sigmoid(hid)
    o_ref[0, 0, 0] = _dot(hid.astype(BF16), w2_ref[0]).astype(BF16)


def _compress(a2, pos8, w1, w2):
    _, b, g, nc, width = a2.shape
    return pl.pallas_call(
        _compress_kernel,
        grid=(2, b, g),
        in_specs=[pl.BlockSpec((1, 1, 1, nc, width), lambda s, bb, gg: (s, bb, gg, 0, 0)),
                  pl.BlockSpec((1, 8, width * 2), lambda s, bb, gg: (s, 0, 0)),
                  pl.BlockSpec((1, width * 2, CMP_HIDDEN), lambda s, bb, gg: (s, 0, 0)),
                  pl.BlockSpec((1, CMP_HIDDEN, NSA_HEAD_DIM), lambda s, bb, gg: (s, 0, 0))],
        out_specs=pl.BlockSpec((1, 1, 1, nc, NSA_HEAD_DIM), lambda s, bb, gg: (s, bb, gg, 0, 0)),
        out_shape=jax.ShapeDtypeStruct((2, b, g, nc, NSA_HEAD_DIM), BF16),
        compiler_params=pltpu.CompilerParams(dimension_semantics=("arbitrary",) * 3),
        name="nsa_compress",
    )(a2, pos8, w1, w2)


def _gate_cols(gates, branch, pair):
    c = branch * 4 + 2 * pair
    return gates[:, c:c + 1], gates[:, c + 1:c + 2]


def _cmp_select_kernel(q_ref, kca_ref, kcb_ref, vc_ref, gate_ref, ovl_ref, a0_ref, bias_ref,
                       *, qt, n_blocks, n_sel):
    t0 = pl.program_id(2) * qt
    nc = kca_ref.shape[2]
    lane = lax.broadcasted_iota(jnp.int32, (qt, LANES), 1)
    low = lane < NSA_HEAD_DIM
    tpos = t0 + lax.broadcasted_iota(jnp.int32, (qt, nc), 0)
    kidx = lax.broadcasted_iota(jnp.int32, (qt, nc), 1)
    valid = kidx * CMP_STRIDE + (CMP_BLOCK - 1) <= tpos
    gates = gate_ref[0]
    kca = kca_ref[0, 0]
    kcb = kcb_ref[0, 0]
    vc = vc_ref[0, 0]
    zero = jnp.zeros((qt, LANES), BF16)
    psum = jnp.zeros((qt, nc), F32)
    for pair in range(2):
        qp = q_ref[0, :, pair * LANES:(pair + 1) * LANES]
        outs = []
        for par in range(2):
            qm = jnp.where(low if par == 0 else jnp.logical_not(low), qp, zero)
            s = _nt_dot(qm, kca if par == 0 else kcb)
            s = jnp.where(valid, s, NEG)
            m = jnp.max(s, axis=1, keepdims=True)
            e = jnp.exp(s - m)
            l = jnp.sum(e, axis=1, keepdims=True)
            p = jnp.where(valid, e / l, 0.0)
            psum = psum + p
            outs.append(_dot(p.astype(BF16), vc))
        ge, go = _gate_cols(gates, 0, pair)
        a0_ref[0, :, pair * LANES:(pair + 1) * LANES] = jnp.where(low, outs[0] * ge, outs[1] * go)

    ovl = ovl_ref[...]
    hi = psum.astype(BF16)
    r1 = psum - hi.astype(F32)
    mid = r1.astype(BF16)
    lo = (r1 - mid.astype(F32)).astype(BF16)
    imp = _dot(hi, ovl) + _dot(mid, ovl) + _dot(lo, ovl)
    tq = t0 + lax.broadcasted_iota(jnp.int32, (qt, LANES), 0)
    cur = lax.shift_right_logical(tq, 6)
    forced = (lane == 0) | (lane == cur) | (lane == cur - 1)
    imp = jnp.where(lane > cur, NEG, jnp.where(forced, imp + FORCE, imp))

    half_rows = LANES // 2
    for c in range(qt // LANES):
        x = imp[c * LANES:(c + 1) * LANES, :].T[:half_rows, :]
        nio = lax.broadcasted_iota(jnp.int32, (half_rows, LANES), 0)
        cnt = jnp.zeros((half_rows, LANES), jnp.int32)
        for mblk in range(n_blocks):
            row = x[mblk:mblk + 1, :]
            beats = (row > x) | ((row == x) & (nio > mblk))
            cnt = cnt + jnp.where(beats, 1, 0)
        bias_t = jnp.where(cnt < n_sel, 0.0, NEG).astype(F32)
        both = jnp.concatenate([bias_t, bias_t], axis=0)
        bias_ref[0, 0, c * LANES:(c + 1) * LANES, :] = both.T.astype(BF16)


def _cmp_select(main3, kca, kcb, vc2, gates3, ovl, qt):
    b, t, _ = main3.shape
    g = NSA_KV_HEADS
    nc = kca.shape[2]
    n_blocks = t // SLC_BLOCK
    assert n_blocks <= LANES // 2 and SLC_BLOCK == 64
    kern = functools.partial(_cmp_select_kernel, qt=qt, n_blocks=n_blocks,
                             n_sel=min(SLC_TOPK, n_blocks))
    kv_spec = pl.BlockSpec((1, 1, nc, LANES), lambda bb, gg, tt: (bb, gg, 0, 0))
    return pl.pallas_call(
        kern,
        grid=(b, g, t // qt),
        in_specs=[pl.BlockSpec((1, qt, MXU_DIM), lambda bb, gg, tt: (bb, tt, gg)),
                  kv_spec, kv_spec, kv_spec,
                  pl.BlockSpec((1, qt, LANES), lambda bb, gg, tt: (bb, tt, gg)),
                  pl.BlockSpec((nc, LANES), lambda bb, gg, tt: (0, 0))],
        out_specs=[pl.BlockSpec((1, qt, MXU_DIM), lambda bb, gg, tt: (bb, tt, gg)),
                   pl.BlockSpec((1, 1, qt, LANES), lambda bb, gg, tt: (bb, gg, tt, 0))],
        out_shape=[jax.ShapeDtypeStruct((b, t, NSA_QD), F32),
                   jax.ShapeDtypeStruct((b, g, t, LANES), BF16)],
        compiler_params=pltpu.CompilerParams(dimension_semantics=("arbitrary",) * 3),
        name="nsa_cmp_select",
    )(main3, kca, kcb, vc2, gates3, ovl)


def _flash_update(hidx, s, v4, m_sc, acc_sc, first):
    if first:
        m_new = jnp.max(s, axis=1, keepdims=True)
        p = jnp.exp(s - m_new)
        acc_sc[hidx] = _dot(p.astype(BF16), v4)
    else:
        m_old = m_sc[hidx]
        m_new = jnp.maximum(m_old, jnp.max(s, axis=1, keepdims=True))
        p = jnp.exp(s - m_new)
        acc_sc[hidx] = acc_sc[hidx] * jnp.exp(m_old - m_new) + _dot(p.astype(BF16), v4)
    m_sc[hidx] = m_new


def _finish_pairs(acc_sc, gates, branch, prev, low):
    tiles = []
    for pair in range(2):
        ge, go = _gate_cols(gates, branch, pair)
        ae = acc_sc[2 * pair]
        ao = acc_sc[2 * pair + 1]
        oe = ae[:, :LANES] / ae[:, LANES:]
        oo = ao[:, :LANES] / ao[:, LANES:]
        tiles.append(prev[:, pair * LANES:(pair + 1) * LANES] + jnp.where(low, oe * ge, oo * go))
    return tiles


def _slc_attn_kernel(q_ref, bias_ref, ka_ref, kb_ref, v_ref, gate_ref, a0_ref, o_ref,
                     qaug_sc, m_sc, acc_sc, *, qt, kt):
    t0 = pl.program_id(2) * qt
    lane = lax.broadcasted_iota(jnp.int32, (qt, LANES), 1)
    low = lane < NSA_HEAD_DIM
    bias = bias_ref[0, 0]
    for pair in range(2):
        qp = q_ref[0, :, pair * LANES:(pair + 1) * LANES]
        qaug_sc[2 * pair] = jnp.where(low, qp, bias)
        qaug_sc[2 * pair + 1] = jnp.where(low, bias, qp)
    m_sc[...] = jnp.full(m_sc.shape, NEG, F32)
    acc_sc[...] = jnp.zeros_like(acc_sc)

    def tile(kidx, masked):
        ks = pl.multiple_of(kidx * kt, kt)
        ka = ka_ref[0, 0, pl.ds(ks, kt), :]
        kb = kb_ref[0, 0, pl.ds(ks, kt), :]
        v4 = v_ref[0, 0, pl.ds(ks, kt), :]
        if masked:
            kpos = ks + lax.broadcasted_iota(jnp.int32, (qt, kt), 1)
            tpos = t0 + lax.broadcasted_iota(jnp.int32, (qt, kt), 0)
            keep = kpos <= tpos
        for hidx in range(4):
            s = _nt_dot(qaug_sc[hidx], ka if hidx % 2 == 0 else kb)
            if masked:
                s = jnp.where(keep, s, NEG)
            _flash_update(hidx, s, v4, m_sc, acc_sc, first=False)

    n_full = (t0 + 1) // kt
    n_all = (t0 + qt - 1) // kt + 1

    def full_body(kidx, carry):
        tile(kidx, False)
        return carry

    def masked_body(kidx, carry):
        tile(kidx, True)
        return carry

    lax.fori_loop(0, n_full, full_body, 0)
    lax.fori_loop(n_full, n_all, masked_body, 0)

    tiles = _finish_pairs(acc_sc, gate_ref[0], 1, a0_ref[0], low)
    for pair in range(2):
        o_ref[0, :, pair * LANES:(pair + 1) * LANES] = tiles[pair]


def _slc_attn(main3, bias2, ka, kb, v4, gates3, a0, qt, kt):
    b, t, _ = main3.shape
    g = NSA_KV_HEADS
    kern = functools.partial(_slc_attn_kernel, qt=qt, kt=kt)
    kspec = pl.BlockSpec((1, 1, t, LANES), lambda bb, gg, tt: (bb, gg, 0, 0))
    return pl.pallas_call(
        kern,
        grid=(b, g, t // qt),
        in_specs=[pl.BlockSpec((1, qt, MXU_DIM), lambda bb, gg, tt: (bb, tt, gg)),
                  pl.BlockSpec((1, 1, qt, LANES), lambda bb, gg, tt: (bb, gg, tt, 0)),
                  kspec, kspec,
                  pl.BlockSpec((1, 1, t, 2 * LANES), lambda bb, gg, tt: (bb, gg, 0, 0)),
                  pl.BlockSpec((1, qt, LANES), lambda bb, gg, tt: (bb, tt, gg)),
                  pl.BlockSpec((1, qt, MXU_DIM), lambda bb, gg, tt: (bb, tt, gg))],
        out_specs=pl.BlockSpec((1, qt, MXU_DIM), lambda bb, gg, tt: (bb, tt, gg)),
        out_shape=jax.ShapeDtypeStruct((b, t, NSA_QD), F32),
        scratch_shapes=[pltpu.VMEM((4, qt, LANES), BF16),
                        pltpu.VMEM((4, qt, 1), F32),
                        pltpu.VMEM((4, qt, 2 * LANES), F32)],
        compiler_params=pltpu.CompilerParams(dimension_semantics=("arbitrary",) * 3),
        name="nsa_slc_attn",
    )(main3, bias2, ka, kb, v4, gates3, a0)


def _win_attn_kernel(q_ref, ka_ref, kb_ref, v_ref, gate_ref, a1_ref, o_ref, qm_sc, m_sc, acc_sc,
                     *, qt, n_back):
    qi = pl.program_id(2)
    lane = lax.broadcasted_iota(jnp.int32, (qt, LANES), 1)
    low = lane < NSA_HEAD_DIM
    zero = jnp.zeros((qt, LANES), BF16)
    for pair in range(2):
        qp = q_ref[0, :, pair * LANES:(pair + 1) * LANES]
        qm_sc[2 * pair] = jnp.where(low, qp, zero)
        qm_sc[2 * pair + 1] = jnp.where(low, zero, qp)
    ri = lax.broadcasted_iota(jnp.int32, (qt, qt), 0)
    ci = lax.broadcasted_iota(jnp.int32, (qt, qt), 1)

    def tile(back, first):
        ks = pl.multiple_of((qi - back) * qt, qt)
        ka = ka_ref[0, 0, pl.ds(ks, qt), :]
        kb = kb_ref[0, 0, pl.ds(ks, qt), :]
        v4 = v_ref[0, 0, pl.ds(ks, qt), :]
        for hidx in range(4):
            s = _nt_dot(qm_sc[hidx], ka if hidx % 2 == 0 else kb)
            if back == 0:
                s = jnp.where(ci <= ri, s, NEG)
            elif back == n_back:
                s = jnp.where(ci > ri, s, NEG)
            _flash_update(hidx, s, v4, m_sc, acc_sc, first=first)

    tile(0, True)
    for back in range(1, n_back + 1):
        @pl.when(qi >= back)
        def _(back=back):
            tile(back, False)

    tiles = _finish_pairs(acc_sc, gate_ref[0], 2, a1_ref[0], low)
    for pair in range(2):
        o_ref[0, :, pair * LANES:(pair + 1) * LANES] = tiles[pair].astype(BF16)


def _win_attn(main3, ka, kb, v4, gates3, a1, qt):
    b, t, _ = main3.shape
    g = NSA_KV_HEADS
    assert WINDOW % qt == 0
    kern = functools.partial(_win_attn_kernel, qt=qt, n_back=WINDOW // qt)
    kspec = pl.BlockSpec((1, 1, t, LANES), lambda bb, gg, tt: (bb, gg, 0, 0))
    return pl.pallas_call(
        kern,
        grid=(b, g, t // qt),
        in_specs=[pl.BlockSpec((1, qt, MXU_DIM), lambda bb, gg, tt: (bb, tt, gg)),
                  kspec, kspec,
                  pl.BlockSpec((1, 1, t, 2 * LANES), lambda bb, gg, tt: (bb, gg, 0, 0)),
                  pl.BlockSpec((1, qt, LANES), lambda bb, gg, tt: (bb, tt, gg)),
                  pl.BlockSpec((1, qt, MXU_DIM), lambda bb, gg, tt: (bb, tt, gg))],
        out_specs=pl.BlockSpec((1, qt, MXU_DIM), lambda bb, gg, tt: (bb, tt, gg)),
        out_shape=jax.ShapeDtypeStruct((b, t, NSA_QD), BF16),
        scratch_shapes=[pltpu.VMEM((4, qt, LANES), BF16),
                        pltpu.VMEM((4, qt, 1), F32),
                        pltpu.VMEM((4, qt, 2 * LANES), F32)],
        compiler_params=pltpu.CompilerParams(dimension_semantics=("arbitrary",) * 3),
        name="nsa_win_attn",
    )(main3, ka, kb, v4, gates3, a1)


def _ret_rope_consts():
    half = RET_QK_DIM // 2
    inv_freq = RET_ROPE_THETA ** (-2.0 * jnp.arange(half, dtype=F32) / RET_QK_DIM)
    rows = jnp.zeros((8, LANES), F32)
    return rows.at[0].set(inv_freq).at[1].set(1.0)


def _nsa_rope_consts():
    half = NSA_ROPE_DIM // 2
    inv_freq = NSA_ROPE_THETA ** (-2.0 * jnp.arange(half, dtype=F32) / NSA_ROPE_DIM)
    r = np.arange(LANES) % NSA_HEAD_DIM
    rot = r < NSA_ROPE_DIM
    freq = jnp.where(rot, inv_freq[r % half], 0.0)
    mask_a = ((r >= half) & rot).astype(np.float32)
    mask_b = -(r < half).astype(np.float32)
    rows = jnp.zeros((8, LANES), F32)
    return rows.at[0].set(freq).at[1].set(mask_a).at[2].set(mask_b)


def _nsa_weight_columns():
    kv0 = NSA_QD
    cols = list(range(NSA_QD))
    for kvsel in range(2):
        for branch in range(3):
            base = kv0 + (branch * 2 + kvsel) * NSA_KVD
            cols += list(range(base, base + NSA_KVD))
    gate0 = NSA_QD + 6 * NSA_KVD
    hg = NSA_HEADS // NSA_KV_HEADS
    for g in range(NSA_KV_HEADS):
        lanes = [-1] * LANES
        for branch in range(3):
            for hh in range(hg):
                lanes[branch * 4 + hh] = gate0 + (g * hg + hh) * 3 + branch
        cols += lanes
    return np.asarray(cols, np.int32)


def _group_major(a, b, t):
    return a.reshape(b, t, NSA_KV_HEADS, NSA_HEAD_DIM).transpose(0, 2, 1, 3)


def _pad_pair(k):
    z = jnp.zeros_like(k)
    return jnp.concatenate([k, z], axis=-1), jnp.concatenate([z, k], axis=-1)


def _retention_layer(x2d, b, t, gain, w_in, w_out, cos, sin, tm):
    proj = _ret_proj(x2d, gain.reshape(1, D_MODEL), w_in.astype(BF16), cos, sin, tm)
    o = _retention(proj.reshape(b, t, RET_IN), min(t, 1024))
    return _matmul_res(o.reshape(b * t, RET_HEADS * RET_V_DIM), w_out.astype(BF16), x2d, min(tm, 512))


def _nsa_layer(x2d, b, t, gain, w_in, cmp_pos, cmp_w1, cmp_w2, w_out, tabs, tm):
    g = NSA_KV_HEADS
    cols = _nsa_weight_columns()
    w_ext = jnp.concatenate([w_in, jnp.zeros((D_MODEL, 1), w_in.dtype)], axis=1)
    w_re = jnp.take(w_ext, jnp.asarray(np.where(cols < 0, w_in.shape[1], cols)), axis=1).astype(BF16)
    main, gates = _nsa_proj(x2d, gain.reshape(1, D_MODEL), w_re, *tabs, tm)

    def kv(i):
        lo = NSA_QD + i * NSA_KVD
        return _group_major(main[:, lo:lo + NSA_KVD], b, t)

    k_cmp_tok, k_slc, k_win, v_cmp_tok, v_slc, v_win = (kv(i) for i in range(6))

    nc = t // CMP_STRIDE
    a2 = jnp.stack([k_cmp_tok, v_cmp_tok]).reshape(2, b, g, nc, CMP_STRIDE * NSA_HEAD_DIM)
    pos8 = jnp.broadcast_to(cmp_pos.reshape(2, 1, CMP_BLOCK * NSA_HEAD_DIM),
                            (2, 8, CMP_BLOCK * NSA_HEAD_DIM)).astype(BF16)
    cmp = _compress(a2, pos8, cmp_w1.astype(BF16), cmp_w2.astype(BF16))
    kca, kcb = _pad_pair(cmp[0])
    vc2 = jnp.concatenate([cmp[1], cmp[1]], axis=-1)

    n_blocks = t // SLC_BLOCK
    kk = np.arange(nc)[:, None]
    nn = np.arange(LANES)[None, :]
    ovl = ((kk * CMP_STRIDE < nn * SLC_BLOCK + SLC_BLOCK) & (kk * CMP_STRIDE + CMP_BLOCK > nn * SLC_BLOCK)
           & (nn < n_blocks) & (kk < nc - 1))
    ovl = jnp.asarray(ovl, BF16)

    main3 = main.reshape(b, t, NSA_MAIN)
    gates3 = gates.reshape(b, t, NSA_GATE_PAD)
    qt = min(t, 256)
    a0, bias2 = _cmp_select(main3, kca, kcb, vc2, gates3, ovl, qt)

    onehot = jnp.asarray(np.arange(t)[:, None] // SLC_BLOCK == np.arange(NSA_HEAD_DIM)[None, :], BF16)
    onehot = jnp.broadcast_to(onehot, (b, g, t, NSA_HEAD_DIM))
    ones = jnp.ones((b, g, t, LANES), BF16)
    ksa = jnp.concatenate([k_slc, onehot], axis=-1)
    ksb = jnp.concatenate([onehot, k_slc], axis=-1)
    vs4 = jnp.concatenate([v_slc, v_slc, ones], axis=-1)
    a1 = _slc_attn(main3, bias2, ksa, ksb, vs4, gates3, a0, qt, min(t, 512))

    kwa, kwb = _pad_pair(k_win)
    vw4 = jnp.concatenate([v_win, v_win, ones], axis=-1)
    attn = _win_attn(main3, kwa, kwb, vw4, gates3, a1, qt)
    return _matmul_res(attn.reshape(b * t, NSA_QD), w_out.astype(BF16), x2d, min(tm, 512))


def _ffn_layer(x2d, gain, w_gu, w_down, tm, final_gain=None):
    hmid = _ffn_gu(x2d, gain.reshape(1, D_MODEL), w_gu.astype(BF16), tm)
    fg = None if final_gain is None else final_gain.reshape(1, D_MODEL)
    return _matmul_res(hmid, w_down.astype(BF16), x2d, min(tm, 512), final_gain=fg)


def kernel(x, positions, norm_mix, norm_ffn, norm_final, ret_w_in, ret_w_out, nsa_w_in, nsa_cmp_pos,
           nsa_cmp_w1, nsa_cmp_w2, nsa_w_out, ffn_w_gu, ffn_w_down):
    b, t, _ = x.shape
    m = b * t
    tm = min(m, 1024)
    x2d = x.reshape(m, D_MODEL)
    pos_col = positions.reshape(m, 1).astype(jnp.int32)
    ret_cos, ret_sin, _ = _rope_tables(pos_col, _ret_rope_consts(), tm)
    nsa_tabs = _rope_tables(pos_col, _nsa_rope_consts(), tm)
    for i in range(DEPTH):
        j = i // 2
        if i % 2 == 0:
            x2d = _retention_layer(x2d, b, t, norm_mix[i], ret_w_in[j], ret_w_out[j], ret_cos, ret_sin, tm)
        else:
            x2d = _nsa_layer(x2d, b, t, norm_mix[i], nsa_w_in[j], nsa_cmp_pos[j], nsa_cmp_w1[j],
                             nsa_cmp_w2[j], nsa_w_out[j], nsa_tabs, tm)
        x2d = _ffn_layer(x2d, norm_ffn[i], ffn_w_gu[i], ffn_w_down[i], tm,
                         final_gain=norm_final if i == DEPTH - 1 else None)
    return x2d.reshape(b, t, D_MODEL)
```

```python
import functools

import numpy as np
import jax
import jax.numpy as jnp
from jax import lax
from jax.experimental import pallas as pl
from jax.experimental.pallas import tpu as pltpu

F32 = jnp.float32
BF16 = jnp.bfloat16

D_MODEL = 1024
DEPTH = 4
NORM_EPS = 1e-6
NEG = -1e30

RET_HEADS = 4
RET_QK_DIM = 256
RET_V_DIM = 512
RET_ROPE_THETA = 10000.0
RET_IN = 6144
RET_CHUNK = 256

NSA_HEADS = 16
NSA_KV_HEADS = 4
NSA_HEAD_DIM = 64
NSA_ROPE_THETA = 500000.0
NSA_ROPE_DIM = 16
CMP_BLOCK = 32
CMP_STRIDE = 16
CMP_HIDDEN = 256
SLC_BLOCK = 64
SLC_TOPK = 16
WINDOW = 512
FORCE = 1e4
NSA_QD = 1024
NSA_KVD = 256
FFN_HIDDEN = 2816

LANES = 128
MXU_DIM = 256


def _nt_dot(a, b):
    return lax.dot_general(a, b, (((1,), (1,)), ((), ())), preferred_element_type=F32)


def _dot(a, b):
    return jnp.dot(a, b, preferred_element_type=F32)


def _rms_to_bf16(x, gain):
    ms = jnp.mean(x * x, axis=-1, keepdims=True)
    return (x * lax.rsqrt(ms + NORM_EPS) * gain).astype(BF16)


def _rope_tables_kernel(pos_ref, c_ref, cos_ref, sa_ref, sb_ref):
    ang = pos_ref[...].astype(F32) * c_ref[0:1, :]
    s = jnp.sin(ang)
    cos_ref[...] = jnp.cos(ang)
    sa_ref[...] = s * c_ref[1:2, :]
    sb_ref[...] = s * c_ref[2:3, :]


def _rope_tables(pos_col, consts, tm):
    m = pos_col.shape[0]
    out = jax.ShapeDtypeStruct((m, LANES), F32)
    return pl.pallas_call(
        _rope_tables_kernel,
        grid=(m // tm,),
        in_specs=[pl.BlockSpec((tm, 1), lambda i: (i, 0)),
                  pl.BlockSpec((8, LANES), lambda i: (0, 0))],
        out_specs=[pl.BlockSpec((tm, LANES), lambda i: (i, 0))] * 3,
        out_shape=[out, out, out],
        name="rope_tables",
    )(pos_col, consts)


def _ret_proj_kernel(x_ref, gain_ref, w_ref, cos_ref, sin_ref, o_ref, hn_ref):
    j = pl.program_id(1)

    @pl.when(j == 0)
    def _():
        hn_ref[...] = _rms_to_bf16(x_ref[...], gain_ref[...])

    acc = _dot(hn_ref[...], w_ref[...])

    @pl.when(j < 8)
    def _():
        scale = jnp.where(j >= 4, RET_QK_DIM ** -0.5, 1.0).astype(F32)
        x1 = acc[:, :LANES]
        x2 = acc[:, LANES:]
        c = cos_ref[...]
        s = sin_ref[...]
        o_ref[:, :LANES] = ((x1 * c - x2 * s) * scale).astype(BF16)
        o_ref[:, LANES:] = ((x2 * c + x1 * s) * scale).astype(BF16)

    @pl.when((j >= 8) & (j < 16))
    def _():
        o_ref[...] = acc.astype(BF16)

    @pl.when(j >= 16)
    def _():
        o_ref[...] = (acc * jax.nn.sigmoid(acc)).astype(BF16)


def _ret_proj(x2d, gain, w_bf16, cos, sin, tm):
    m = x2d.shape[0]
    tn = MXU_DIM
    return pl.pallas_call(
        _ret_proj_kernel,
        grid=(m // tm, RET_IN // tn),
        in_specs=[pl.BlockSpec((tm, D_MODEL), lambda i, j: (i, 0)),
                  pl.BlockSpec((1, D_MODEL), lambda i, j: (0, 0)),
                  pl.BlockSpec((D_MODEL, tn), lambda i, j: (0, j)),
                  pl.BlockSpec((tm, LANES), lambda i, j: (i, 0)),
                  pl.BlockSpec((tm, LANES), lambda i, j: (i, 0))],
        out_specs=pl.BlockSpec((tm, tn), lambda i, j: (i, j)),
        out_shape=jax.ShapeDtypeStruct((m, RET_IN), BF16),
        scratch_shapes=[pltpu.VMEM((tm, D_MODEL), BF16)],
        compiler_params=pltpu.CompilerParams(dimension_semantics=("arbitrary", "arbitrary")),
        name="ret_proj",
    )(x2d, gain, w_bf16, cos, sin)


def _retention_kernel(q_ref, k_ref, v_ref, g_ref, o_ref, state, intra, qd, kd, *, chunk, n_chunks):
    c_len = chunk
    h = pl.program_id(1)
    t = pl.program_id(2)

    @pl.when(t == 0)
    def _():
        hv = jnp.full((c_len, RET_QK_DIM), h, jnp.int32)
        den = jnp.left_shift(jnp.full((c_len, RET_QK_DIM), 32, jnp.int32), hv).astype(F32)
        lg = jnp.log(1.0 - 1.0 / den)
        ri = lax.broadcasted_iota(jnp.int32, (c_len, RET_QK_DIM), 0)
        ci = lax.broadcasted_iota(jnp.int32, (c_len, RET_QK_DIM), 1)
        diff = (ri - ci).astype(F32)
        intra[...] = jnp.where(diff >= 0, jnp.exp(lg * jnp.maximum(diff, 0.0)), 0.0)
        rf = ri.astype(F32)
        qd[...] = jnp.exp(lg * (rf + 1.0))
        kd[...] = jnp.exp(lg * (c_len - 1.0 - rf))
        state[...] = jnp.zeros_like(state)

    chunk_decay = qd[c_len - 1:c_len, 0:1]
    for c in range(n_chunks):
        sl = pl.ds(c * c_len, c_len)
        qc = q_ref[0, sl, :]
        kc = k_ref[0, sl, :]
        vc = v_ref[0, sl, :]
        s = _nt_dot(qc, kc) * intra[...]
        st = state[...]
        qdec = (qc.astype(F32) * qd[...]).astype(BF16)
        o = _dot(s.astype(BF16), vc) + _dot(qdec, st.astype(BF16))
        kdec = (kc.astype(F32) * kd[...]).astype(BF16)
        state[...] = st * chunk_decay + lax.dot_general(
            kdec, vc, (((0,), (0,)), ((), ())), preferred_element_type=F32)
        ms = jnp.mean(o * o, axis=-1, keepdims=True)
        y = o * lax.rsqrt(ms + NORM_EPS)
        o_ref[0, sl, :] = (y * g_ref[0, sl, :].astype(F32)).astype(BF16)


def _retention(proj3, tq):
    b, t, _ = proj3.shape
    chunk = RET_CHUNK
    assert RET_CHUNK == RET_QK_DIM and tq % chunk == 0 and t % tq == 0
    kern = functools.partial(_retention_kernel, chunk=chunk, n_chunks=tq // chunk)
    nq = D_MODEL // RET_QK_DIM
    return pl.pallas_call(
        kern,
        grid=(b, RET_HEADS, t // tq),
        in_specs=[pl.BlockSpec((1, tq, RET_QK_DIM), lambda bb, h, tt: (bb, tt, h)),
                  pl.BlockSpec((1, tq, RET_QK_DIM), lambda bb, h, tt: (bb, tt, nq + h)),
                  pl.BlockSpec((1, tq, RET_V_DIM), lambda bb, h, tt: (bb, tt, nq + h)),
                  pl.BlockSpec((1, tq, RET_V_DIM), lambda bb, h, tt: (bb, tt, 2 * nq + h))],
        out_specs=pl.BlockSpec((1, tq, RET_V_DIM), lambda bb, h, tt: (bb, tt, h)),
        out_shape=jax.ShapeDtypeStruct((b, t, RET_HEADS * RET_V_DIM), BF16),
        scratch_shapes=[pltpu.VMEM((RET_QK_DIM, RET_V_DIM), F32),
                        pltpu.VMEM((chunk, RET_QK_DIM), F32),
                        pltpu.VMEM((chunk, RET_QK_DIM), F32),
                        pltpu.VMEM((chunk, RET_QK_DIM), F32)],
        compiler_params=pltpu.CompilerParams(dimension_semantics=("arbitrary",) * 3),
        name="retention",
    )(proj3, proj3, proj3, proj3)


def _matmul_res_kernel(a_ref, w_ref, res_ref, *rest, final_norm):
    if final_norm:
        gain_ref, o_ref = rest
    else:
        (o_ref,) = rest
    y = res_ref[...] + _dot(a_ref[...], w_ref[...])
    if final_norm:
        ms = jnp.mean(y * y, axis=-1, keepdims=True)
        y = y * lax.rsqrt(ms + NORM_EPS) * gain_ref[...]
    o_ref[...] = y


def _matmul_res(a, w_bf16, res, tm, final_gain=None):
    m, k = a.shape
    n = w_bf16.shape[1]
    in_specs = [pl.BlockSpec((tm, k), lambda i: (i, 0)),
                pl.BlockSpec((k, n), lambda i: (0, 0)),
                pl.BlockSpec((tm, n), lambda i: (i, 0))]
    args = [a, w_bf16, res]
    if final_gain is not None:
        in_specs.append(pl.BlockSpec((1, n), lambda i: (0, 0)))
        args.append(final_gain)
    return pl.pallas_call(
        functools.partial(_matmul_res_kernel, final_norm=final_gain is not None),
        grid=(m // tm,),
        in_specs=in_specs,
        out_specs=pl.BlockSpec((tm, n), lambda i: (i, 0)),
        out_shape=jax.ShapeDtypeStruct((m, n), F32),
        compiler_params=pltpu.CompilerParams(dimension_semantics=("arbitrary",)),
        name="matmul_res",
    )(*args)


def _ffn_gu_kernel(x_ref, gain_ref, wg_ref, wu_ref, o_ref, hn_ref):
    @pl.when(pl.program_id(1) == 0)
    def _():
        hn_ref[...] = _rms_to_bf16(x_ref[...], gain_ref[...])

    hn = hn_ref[...]
    a = _dot(hn, wg_ref[...])
    b = _dot(hn, wu_ref[...])
    o_ref[...] = (a * jax.nn.sigmoid(a) * b).astype(BF16)


def _ffn_gu(x2d, gain, w_gu_bf16, tm):
    m = x2d.shape[0]
    tn = MXU_DIM
    nj = FFN_HIDDEN // tn
    return pl.pallas_call(
        _ffn_gu_kernel,
        grid=(m // tm, nj),
        in_specs=[pl.BlockSpec((tm, D_MODEL), lambda i, j: (i, 0)),
                  pl.BlockSpec((1, D_MODEL), lambda i, j: (0, 0)),
                  pl.BlockSpec((D_MODEL, tn), lambda i, j: (0, j)),
                  pl.BlockSpec((D_MODEL, tn), lambda i, j: (0, j + nj))],
        out_specs=pl.BlockSpec((tm, tn), lambda i, j: (i, j)),
        out_shape=jax.ShapeDtypeStruct((m, FFN_HIDDEN), BF16),
        scratch_shapes=[pltpu.VMEM((tm, D_MODEL), BF16)],
        compiler_params=pltpu.CompilerParams(dimension_semantics=("arbitrary", "arbitrary")),
        name="ffn_gate_up",
    )(x2d, gain, w_gu_bf16, w_gu_bf16)


NSA_MAIN = NSA_QD + 6 * NSA_KVD
NSA_GATE_PAD = NSA_KV_HEADS * LANES
N_MAIN_BLOCKS = NSA_MAIN // MXU_DIM
Q_SCALE_LOG2 = NSA_HEAD_DIM ** -0.5 * float(np.log2(np.e))


def _partial_rotary(a, c, sa, sb):
    return a * c + pltpu.roll(a, 8, 1) * sa + pltpu.roll(a, LANES - 8, 1) * sb


def _nsa_proj_kernel(x_ref, gain_ref, w_ref, c_ref, sa_ref, sb_ref, o_ref, gate_ref, hn_ref):
    j = pl.program_id(1)

    @pl.when(j == 0)
    def _():
        hn_ref[...] = _rms_to_bf16(x_ref[...], gain_ref[...])

    acc = _dot(hn_ref[...], w_ref[...])

    @pl.when(j < 7)
    def _():
        scale = jnp.where(j < 4, Q_SCALE_LOG2, 1.0).astype(F32)
        c = c_ref[...]
        sa = sa_ref[...]
        sb = sb_ref[...]
        for half in range(2):
            sl = slice(half * LANES, (half + 1) * LANES)
            o_ref[:, sl] = (_partial_rotary(acc[:, sl], c, sa, sb) * scale).astype(BF16)

    @pl.when((j >= 7) & (j < N_MAIN_BLOCKS))
    def _():
        o_ref[...] = acc.astype(BF16)

    @pl.when(j >= N_MAIN_BLOCKS)
    def _():
        gate_ref[...] = jax.nn.sigmoid(acc)


def _nsa_proj(x2d, gain, w_bf16, c, sa, sb, tm):
    m = x2d.shape[0]
    tn = MXU_DIM
    nj = (NSA_MAIN + NSA_GATE_PAD) // tn
    last = N_MAIN_BLOCKS - 1
    return pl.pallas_call(
        _nsa_proj_kernel,
        grid=(m // tm, nj),
        in_specs=[pl.BlockSpec((tm, D_MODEL), lambda i, j: (i, 0)),
                  pl.BlockSpec((1, D_MODEL), lambda i, j: (0, 0)),
                  pl.BlockSpec((D_MODEL, tn), lambda i, j: (0, j)),
                  pl.BlockSpec((tm, LANES), lambda i, j: (i, 0)),
                  pl.BlockSpec((tm, LANES), lambda i, j: (i, 0)),
                  pl.BlockSpec((tm, LANES), lambda i, j: (i, 0))],
        out_specs=[pl.BlockSpec((tm, tn), lambda i, j: (i, jnp.minimum(j, last))),
                   pl.BlockSpec((tm, tn), lambda i, j: (i, jnp.maximum(j - N_MAIN_BLOCKS, 0)))],
        out_shape=[jax.ShapeDtypeStruct((m, NSA_MAIN), BF16),
                   jax.ShapeDtypeStruct((m, NSA_GATE_PAD), F32)],
        scratch_shapes=[pltpu.VMEM((tm, D_MODEL), BF16)],
        compiler_params=pltpu.CompilerParams(dimension_semantics=("arbitrary", "arbitrary")),
        name="nsa_proj",
    )(x2d, gain, w_bf16, c, sa, sb)


def _compress_kernel(a_ref, pos_ref, w1_ref, w2_ref, o_ref):
    a = a_ref[0, 0, 0]
    nc = a.shape[0]
    half = CMP_STRIDE * NSA_HEAD_DIM
    upper = _dot(a, w1_ref[0, :half, :])
    lower = _dot(a, w1_ref[0, half:, :])
    cpos = _dot(pos_ref[0], w1_ref[0])
    hid = upper + pltpu.roll(lower, nc - 1, 0) + cpos[0:1, :]
    hid = hid * jax.nn.sigmoid(hid)
    o_ref[0, 0, 0] = _dot(hid.astype(BF16), w2_ref[0]).astype(BF16)


def _compress(a2, pos8, w1, w2):
    _, b, g, nc, width = a2.shape
    return pl.pallas_call(
        _compress_kernel,
        grid=(2, b, g),
        in_specs=[pl.BlockSpec((1, 1, 1, nc, width), lambda s, bb, gg: (s, bb, gg, 0, 0)),
                  pl.BlockSpec((1, 8, width * 2), lambda s, bb, gg: (s, 0, 0)),
                  pl.BlockSpec((1, width * 2, CMP_HIDDEN), lambda s, bb, gg: (s, 0, 0)),
                  pl.BlockSpec((1, CMP_HIDDEN, NSA_HEAD_DIM), lambda s, bb, gg: (s, 0, 0))],
        out_specs=pl.BlockSpec((1, 1, 1, nc, NSA_HEAD_DIM), lambda s, bb, gg: (s, bb, gg, 0, 0)),
        out_shape=jax.ShapeDtypeStruct((2, b, g, nc, NSA_HEAD_DIM), BF16),
        compiler_params=pltpu.CompilerParams(dimension_semantics=("arbitrary",) * 3),
        name="nsa_compress",
    )(a2, pos8, w1, w2)


def _gate_cols(gates, branch, pair):
    c = branch * 4 + 2 * pair
    return gates[:, c:c + 1], gates[:, c + 1:c + 2]


def _cmp_select_kernel(q_ref, kca_ref, kcb_ref, vc_ref, gate_ref, ovl_ref, a0_ref, bias_ref,
                       x_sc, cnt_sc, *, qt, n_blocks, n_sel):
    t0 = pl.program_id(2) * qt
    nc = kca_ref.shape[2]
    half = LANES // 2
    lane = lax.broadcasted_iota(jnp.int32, (qt, LANES), 1)
    low = lane < NSA_HEAD_DIM
    top = lax.broadcasted_iota(jnp.int32, (LANES, qt), 0) < half
    kidx = lax.broadcasted_iota(jnp.int32, (nc, qt), 0)
    tpos = t0 + lax.broadcasted_iota(jnp.int32, (nc, qt), 1)
    valid = kidx * CMP_STRIDE + (CMP_BLOCK - 1) <= tpos
    gates = gate_ref[0]
    kca = kca_ref[0, 0]
    kcb = kcb_ref[0, 0]
    vct = vc_ref[0, 0]
    zero = jnp.zeros((qt, LANES), BF16)
    psum = jnp.zeros((nc, qt), F32)
    for pair in range(2):
        qp = q_ref[0, :, pair * LANES:(pair + 1) * LANES]
        outs = []
        for par in range(2):
            qm = jnp.where(low if par == 0 else jnp.logical_not(low), qp, zero)
            st = _nt_dot(kca if par == 0 else kcb, qm)
            st = jnp.where(valid, st, NEG)
            m = jnp.max(st, axis=0, keepdims=True)
            e = jnp.exp2(st - m)
            l = jnp.sum(e, axis=0, keepdims=True)
            p = jnp.where(valid, e / l, 0.0)
            psum = psum + p
            outs.append(_dot(vct, p.astype(BF16)))
        ge, go = _gate_cols(gates, 0, pair)
        pair_t = jnp.where(top, outs[0], outs[1])
        a0_ref[0, :, pair * LANES:(pair + 1) * LANES] = _t_cols(pair_t) * jnp.where(low, ge, go)

    ovl = ovl_ref[...]
    hi = psum.astype(BF16)
    r1 = psum - hi.astype(F32)
    mid = r1.astype(BF16)
    lo = (r1 - mid.astype(F32)).astype(BF16)
    imp = (_dot(ovl, hi) + _dot(ovl, mid) + _dot(ovl, lo))[:half, :]
    nio = lax.broadcasted_iota(jnp.int32, (half, qt), 0)
    cur = lax.shift_right_logical(t0 + lax.broadcasted_iota(jnp.int32, (half, qt), 1), 6)
    forced = (nio == 0) | (nio == cur) | (nio == cur - 1)
    x_sc[...] = jnp.where(nio > cur, NEG, jnp.where(forced, imp + FORCE, imp))
    cnt_sc[...] = jnp.zeros_like(cnt_sc)

    cur_max = lax.shift_right_logical(t0 + qt - 1, 6)
    sub8 = lax.broadcasted_iota(jnp.int32, (8, qt), 0)
    n_groups = half // 8
    for rm in range(-(-n_blocks // 8)):
        @pl.when(rm * 8 <= cur_max)
        def _(rm=rm):
            rows = x_sc[rm * 8:(rm + 1) * 8, :]
            for r in range(n_groups):
                xr = x_sc[r * 8:(r + 1) * 8, :]
                cnt = cnt_sc[r * 8:(r + 1) * 8, :]
                for mm in range(8):
                    row = rows[mm:mm + 1, :]
                    if r < rm:
                        beats = row > xr
                    elif r > rm:
                        beats = row >= xr
                    else:
                        beats = (row > xr) | ((row == xr) & (sub8 > mm))
                    cnt = cnt + jnp.where(beats, 1, 0)
                cnt_sc[r * 8:(r + 1) * 8, :] = cnt
    bias_t = jnp.where(cnt_sc[...] < n_sel, 0.0, NEG).astype(F32)
    bias_ref[0, 0] = _t_cols(jnp.concatenate([bias_t, bias_t], axis=0)).astype(BF16)


def _t_cols(x):
    n = x.shape[1] // LANES
    return jnp.concatenate([x[:, c * LANES:(c + 1) * LANES].T for c in range(n)], axis=0)


def _cmp_select(main3, kca, kcb, vc2, gates3, ovl, qt):
    b, t, _ = main3.shape
    g = NSA_KV_HEADS
    nc = kca.shape[2]
    n_blocks = t // SLC_BLOCK
    assert n_blocks <= LANES // 2 and SLC_BLOCK == 64
    kern = functools.partial(_cmp_select_kernel, qt=qt, n_blocks=n_blocks,
                             n_sel=min(SLC_TOPK, n_blocks))
    kv_spec = pl.BlockSpec((1, 1, nc, LANES), lambda bb, gg, tt: (bb, gg, 0, 0))
    return pl.pallas_call(
        kern,
        grid=(b, g, t // qt),
        in_specs=[pl.BlockSpec((1, qt, MXU_DIM), lambda bb, gg, tt: (bb, tt, gg)),
                  kv_spec, kv_spec,
                  pl.BlockSpec((1, 1, LANES, nc), lambda bb, gg, tt: (bb, gg, 0, 0)),
                  pl.BlockSpec((1, qt, LANES), lambda bb, gg, tt: (bb, tt, gg)),
                  pl.BlockSpec((LANES, nc), lambda bb, gg, tt: (0, 0))],
        out_specs=[pl.BlockSpec((1, qt, MXU_DIM), lambda bb, gg, tt: (bb, tt, gg)),
                   pl.BlockSpec((1, 1, qt, LANES), lambda bb, gg, tt: (bb, gg, tt, 0))],
        out_shape=[jax.ShapeDtypeStruct((b, t, NSA_QD), F32),
                   jax.ShapeDtypeStruct((b, g, t, LANES), BF16)],
        scratch_shapes=[pltpu.VMEM((LANES // 2, qt), F32),
                        pltpu.VMEM((LANES // 2, qt), jnp.int32)],
        compiler_params=pltpu.CompilerParams(dimension_semantics=("arbitrary",) * 3),
        name="nsa_cmp_select",
    )(main3, kca, kcb, vc2, gates3, ovl)


def _flash_update(hidx, st, v_aug, m_sc, acc_sc, first):
    mx = jnp.max(st, axis=0, keepdims=True)
    if first:
        m_new = mx
        p = jnp.exp2(st - m_new)
        acc_sc[hidx] = _dot(v_aug, p.astype(BF16))
    else:
        m_old = m_sc[hidx]
        m_new = jnp.maximum(m_old, mx)
        p = jnp.exp2(st - m_new)
        acc_sc[hidx] = acc_sc[hidx] * jnp.exp2(m_old - m_new) + _dot(v_aug, p.astype(BF16))
    m_sc[hidx] = m_new


def _finish_pairs(acc_sc, gates, branch, prev, low):
    half = LANES // 2
    tiles = []
    qt = prev.shape[0]
    for pair in range(2):
        ge, go = _gate_cols(gates, branch, pair)
        cols = slice(pair * qt, (pair + 1) * qt)
        ae = acc_sc[0][:, cols]
        ao = acc_sc[1][:, cols]
        pair_t = jnp.concatenate([ae[:half] / ae[half:], ao[half:] / ao[:half]], axis=0)
        tiles.append(prev[:, pair * LANES:(pair + 1) * LANES] + _t_cols(pair_t) * jnp.where(low, ge, go))
    return tiles


def _slc_attn_kernel(q_ref, bias_ref, ka_ref, kb_ref, ve_ref, vo_ref, gate_ref, a0_ref, o_ref,
                     qaug_sc, m_sc, acc_sc, sta_sc, stb_sc, *, qt, kt):
    t0 = pl.program_id(2) * qt
    lane = lax.broadcasted_iota(jnp.int32, (qt, LANES), 1)
    low = lane < NSA_HEAD_DIM
    bias = bias_ref[0, 0]
    for pair in range(2):
        qp = q_ref[0, :, pair * LANES:(pair + 1) * LANES]
        qaug_sc[0, pair * qt:(pair + 1) * qt] = jnp.where(low, qp, bias)
        qaug_sc[1, pair * qt:(pair + 1) * qt] = jnp.where(low, bias, qp)
    m_sc[...] = jnp.full(m_sc.shape, NEG, F32)
    acc_sc[...] = jnp.zeros_like(acc_sc)

    def scores(kidx, buf):
        ks = pl.multiple_of(kidx * kt, kt)
        ka = ka_ref[0, 0, pl.ds(ks, kt), :]
        kb = kb_ref[0, 0, pl.ds(ks, kt), :]
        buf[0] = _nt_dot(ka, qaug_sc[0])
        buf[1] = _nt_dot(kb, qaug_sc[1])

    def consume(kidx, buf, masked):
        v_aug = (ve_ref[0, 0, kidx], vo_ref[0, 0, kidx])
        if masked:
            kpos = kidx * kt + lax.broadcasted_iota(jnp.int32, (kt, qt), 0)
            tpos = t0 + lax.broadcasted_iota(jnp.int32, (kt, qt), 1)
            keep = kpos <= tpos
            keep = jnp.concatenate([keep, keep], axis=1)
        for par in range(2):
            st = buf[par]
            if masked:
                st = jnp.where(keep, st, NEG)
            _flash_update(par, st, v_aug[par], m_sc, acc_sc, first=False)

    assert qt <= kt
    last = (t0 + qt - 1) // kt
    scores(0, sta_sc)

    def body(j, carry):
        scores(2 * j + 1, stb_sc)
        consume(2 * j, sta_sc, False)
        scores(2 * j + 2, sta_sc)
        consume(2 * j + 1, stb_sc, False)
        return carry

    lax.fori_loop(0, last // 2, body, 0)

    @pl.when(last % 2 == 0)
    def _():
        consume(last, sta_sc, True)

    @pl.when(last % 2 == 1)
    def _():
        scores(last, stb_sc)
        consume(last - 1, sta_sc, False)
        consume(last, stb_sc, True)

    tiles = _finish_pairs(acc_sc, gate_ref[0], 1, a0_ref[0], low)
    for pair in range(2):
        o_ref[0, :, pair * LANES:(pair + 1) * LANES] = tiles[pair]


def _attn_scratch(qt):
    return [pltpu.VMEM((2, 2 * qt, LANES), BF16),
            pltpu.VMEM((2, 1, 2 * qt), F32),
            pltpu.VMEM((2, LANES, 2 * qt), F32)]


def _slc_attn(main3, bias2, ka, kb, ve, vo, gates3, a0, qt, kt):
    b, t, _ = main3.shape
    g = NSA_KV_HEADS
    kern = functools.partial(_slc_attn_kernel, qt=qt, kt=kt)
    kspec = pl.BlockSpec((1, 1, t, LANES), lambda bb, gg, tt: (bb, gg, 0, 0))
    vspec = pl.BlockSpec((1, 1, t // kt, LANES, kt), lambda bb, gg, tt: (bb, gg, 0, 0, 0))
    return pl.pallas_call(
        kern,
        grid=(b, g, t // qt),
        in_specs=[pl.BlockSpec((1, qt, MXU_DIM), lambda bb, gg, tt: (bb, tt, gg)),
                  pl.BlockSpec((1, 1, qt, LANES), lambda bb, gg, tt: (bb, gg, tt, 0)),
                  kspec, kspec, vspec, vspec,
                  pl.BlockSpec((1, qt, LANES), lambda bb, gg, tt: (bb, tt, gg)),
                  pl.BlockSpec((1, qt, MXU_DIM), lambda bb, gg, tt: (bb, tt, gg))],
        out_specs=pl.BlockSpec((1, qt, MXU_DIM), lambda bb, gg, tt: (bb, tt, gg)),
        out_shape=jax.ShapeDtypeStruct((b, t, NSA_QD), F32),
        scratch_shapes=_attn_scratch(qt) + [pltpu.VMEM((2, kt, 2 * qt), F32)] * 2,
        compiler_params=pltpu.CompilerParams(dimension_semantics=("arbitrary",) * 3),
        name="nsa_slc_attn",
    )(main3, bias2, ka, kb, ve, vo, gates3, a0)


def _win_attn_kernel(q_ref, ka_ref, kb_ref, ve_ref, vo_ref, gate_ref, a1_ref, o_ref, qm_sc, m_sc, acc_sc,
                     *, qt, n_back):
    qi = pl.program_id(2)
    lane = lax.broadcasted_iota(jnp.int32, (qt, LANES), 1)
    low = lane < NSA_HEAD_DIM
    zero = jnp.zeros((qt, LANES), BF16)
    for pair in range(2):
        qp = q_ref[0, :, pair * LANES:(pair + 1) * LANES]
        qm_sc[0, pair * qt:(pair + 1) * qt] = jnp.where(low, qp, zero)
        qm_sc[1, pair * qt:(pair + 1) * qt] = jnp.where(low, zero, qp)
    ri = lax.broadcasted_iota(jnp.int32, (qt, 2 * qt), 0)
    ci = lax.broadcasted_iota(jnp.int32, (qt, 2 * qt), 1)
    ci = jnp.where(ci >= qt, ci - qt, ci)

    keeps, kidxs = [], []
    for back in range(n_back + 1):
        present = qi >= back
        kidxs.append(jnp.maximum(qi - back, 0))
        if back == 0:
            keeps.append(ri <= ci)
        elif back == n_back:
            keeps.append((ri > ci) & present)
        else:
            keeps.append(jnp.broadcast_to(present, (qt, 2 * qt)))
    for par in range(2):
        k_ref = kb_ref if par else ka_ref
        v_ref = vo_ref if par else ve_ref
        sts = []
        for back in range(n_back + 1):
            ks = pl.multiple_of(kidxs[back] * qt, qt)
            st = _nt_dot(k_ref[0, 0, pl.ds(ks, qt), :], qm_sc[par])
            sts.append(jnp.where(keeps[back], st, NEG))
        m = functools.reduce(jnp.maximum, [jnp.max(st, axis=0, keepdims=True) for st in sts])
        acc = None
        for back in range(n_back + 1):
            part = _dot(v_ref[0, 0, kidxs[back]], jnp.exp2(sts[back] - m).astype(BF16))
            acc = part if acc is None else acc + part
        acc_sc[par] = acc

    tiles = _finish_pairs(acc_sc, gate_ref[0], 2, a1_ref[0], low)
    for pair in range(2):
        o_ref[0, :, pair * LANES:(pair + 1) * LANES] = tiles[pair].astype(BF16)


def _win_attn(main3, ka, kb, ve, vo, gates3, a1, qt):
    b, t, _ = main3.shape
    g = NSA_KV_HEADS
    assert WINDOW % qt == 0
    kern = functools.partial(_win_attn_kernel, qt=qt, n_back=WINDOW // qt)
    kspec = pl.BlockSpec((1, 1, t, LANES), lambda bb, gg, tt: (bb, gg, 0, 0))
    vspec = pl.BlockSpec((1, 1, t // qt, LANES, qt), lambda bb, gg, tt: (bb, gg, 0, 0, 0))
    return pl.pallas_call(
        kern,
        grid=(b, g, t // qt),
        in_specs=[pl.BlockSpec((1, qt, MXU_DIM), lambda bb, gg, tt: (bb, tt, gg)),
                  kspec, kspec, vspec, vspec,
                  pl.BlockSpec((1, qt, LANES), lambda bb, gg, tt: (bb, tt, gg)),
                  pl.BlockSpec((1, qt, MXU_DIM), lambda bb, gg, tt: (bb, tt, gg))],
        out_specs=pl.BlockSpec((1, qt, MXU_DIM), lambda bb, gg, tt: (bb, tt, gg)),
        out_shape=jax.ShapeDtypeStruct((b, t, NSA_QD), BF16),
        scratch_shapes=_attn_scratch(qt),
        compiler_params=pltpu.CompilerParams(dimension_semantics=("arbitrary",) * 3),
        name="nsa_win_attn",
    )(main3, ka, kb, ve, vo, gates3, a1)


def _ret_rope_consts():
    half = RET_QK_DIM // 2
    inv_freq = RET_ROPE_THETA ** (-2.0 * jnp.arange(half, dtype=F32) / RET_QK_DIM)
    rows = jnp.zeros((8, LANES), F32)
    return rows.at[0].set(inv_freq).at[1].set(1.0)


def _nsa_rope_consts():
    half = NSA_ROPE_DIM // 2
    inv_freq = NSA_ROPE_THETA ** (-2.0 * jnp.arange(half, dtype=F32) / NSA_ROPE_DIM)
    r = np.arange(LANES) % NSA_HEAD_DIM
    rot = r < NSA_ROPE_DIM
    freq = jnp.where(rot, inv_freq[r % half], 0.0)
    mask_a = ((r >= half) & rot).astype(np.float32)
    mask_b = -(r < half).astype(np.float32)
    rows = jnp.zeros((8, LANES), F32)
    return rows.at[0].set(freq).at[1].set(mask_a).at[2].set(mask_b)


def _nsa_weight_columns():
    kv0 = NSA_QD
    cols = list(range(NSA_QD))
    for kvsel in range(2):
        for branch in range(3):
            base = kv0 + (branch * 2 + kvsel) * NSA_KVD
            cols += list(range(base, base + NSA_KVD))
    gate0 = NSA_QD + 6 * NSA_KVD
    hg = NSA_HEADS // NSA_KV_HEADS
    for g in range(NSA_KV_HEADS):
        lanes = [-1] * LANES
        for branch in range(3):
            for hh in range(hg):
                lanes[branch * 4 + hh] = gate0 + (g * hg + hh) * 3 + branch
        cols += lanes
    return np.asarray(cols, np.int32)


def _group_major(a, b, t):
    return a.reshape(b, t, NSA_KV_HEADS, NSA_HEAD_DIM).transpose(0, 2, 1, 3)


def _pad_pair(k):
    z = jnp.zeros_like(k)
    return jnp.concatenate([k, z], axis=-1), jnp.concatenate([z, k], axis=-1)


def _value_tiles(v, kt):
    b, g, t, d = v.shape
    vt = v.reshape(b, g, t // kt, kt, d).transpose(0, 1, 2, 4, 3)
    ones = jnp.ones_like(vt)
    return jnp.concatenate([vt, ones], axis=3), jnp.concatenate([ones, vt], axis=3)


def _retention_layer(x2d, b, t, gain, w_in, w_out, cos, sin, tm):
    proj = _ret_proj(x2d, gain.reshape(1, D_MODEL), w_in.astype(BF16), cos, sin, tm)
    o = _retention(proj.reshape(b, t, RET_IN), min(t, 1024))
    return _matmul_res(o.reshape(b * t, RET_HEADS * RET_V_DIM), w_out.astype(BF16), x2d, min(tm, 512))


def _nsa_layer(x2d, b, t, gain, w_in, cmp_pos, cmp_w1, cmp_w2, w_out, tabs, tm):
    g = NSA_KV_HEADS
    cols = _nsa_weight_columns()
    w_ext = jnp.concatenate([w_in, jnp.zeros((D_MODEL, 1), w_in.dtype)], axis=1)
    w_re = jnp.take(w_ext, jnp.asarray(np.where(cols < 0, w_in.shape[1], cols)), axis=1).astype(BF16)
    main, gates = _nsa_proj(x2d, gain.reshape(1, D_MODEL), w_re, *tabs, tm)

    def kv(i):
        lo = NSA_QD + i * NSA_KVD
        return _group_major(main[:, lo:lo + NSA_KVD], b, t)

    k_cmp_tok, k_slc, k_win, v_cmp_tok, v_slc, v_win = (kv(i) for i in range(6))

    nc = t // CMP_STRIDE
    a2 = jnp.stack([k_cmp_tok, v_cmp_tok]).reshape(2, b, g, nc, CMP_STRIDE * NSA_HEAD_DIM)
    pos8 = jnp.broadcast_to(cmp_pos.reshape(2, 1, CMP_BLOCK * NSA_HEAD_DIM),
                            (2, 8, CMP_BLOCK * NSA_HEAD_DIM)).astype(BF16)
    cmp = _compress(a2, pos8, cmp_w1.astype(BF16), cmp_w2.astype(BF16))
    kca, kcb = _pad_pair(cmp[0])
    vct = cmp[1].transpose(0, 1, 3, 2)
    vc2 = jnp.concatenate([vct, vct], axis=2)

    n_blocks = t // SLC_BLOCK
    kk = np.arange(nc)[None, :]
    nn = np.arange(LANES)[:, None]
    ovl = ((kk * CMP_STRIDE < nn * SLC_BLOCK + SLC_BLOCK) & (kk * CMP_STRIDE + CMP_BLOCK > nn * SLC_BLOCK)
           & (nn < n_blocks) & (kk < nc - 1))
    ovl = jnp.asarray(ovl, BF16)

    main3 = main.reshape(b, t, NSA_MAIN)
    gates3 = gates.reshape(b, t, NSA_GATE_PAD)
    qt = min(t, 256)
    a0, bias2 = _cmp_select(main3, kca, kcb, vc2, gates3, ovl, qt)

    onehot = jnp.asarray(np.arange(t)[:, None] // SLC_BLOCK == np.arange(NSA_HEAD_DIM)[None, :], BF16)
    onehot = jnp.broadcast_to(onehot, (b, g, t, NSA_HEAD_DIM))
    ksa = jnp.concatenate([k_slc, onehot], axis=-1)
    ksb = jnp.concatenate([onehot, k_slc], axis=-1)
    kt = min(t, 512)
    a1 = _slc_attn(main3, bias2, ksa, ksb, *_value_tiles(v_slc, kt), gates3, a0, qt, kt)

    kwa, kwb = _pad_pair(k_win)
    attn = _win_attn(main3, kwa, kwb, *_value_tiles(v_win, qt), gates3, a1, qt)
    return _matmul_res(attn.reshape(b * t, NSA_QD), w_out.astype(BF16), x2d, min(tm, 512))


def _ffn_layer(x2d, gain, w_gu, w_down, tm, final_gain=None):
    hmid = _ffn_gu(x2d, gain.reshape(1, D_MODEL), w_gu.astype(BF16), tm)
    fg = None if final_gain is None else final_gain.reshape(1, D_MODEL)
    return _matmul_res(hmid, w_down.astype(BF16), x2d, min(tm, 512), final_gain=fg)


def kernel(x, positions, norm_mix, norm_ffn, norm_final, ret_w_in, ret_w_out, nsa_w_in, nsa_cmp_pos,
           nsa_cmp_w1, nsa_cmp_w2, nsa_w_out, ffn_w_gu, ffn_w_down):
    b, t, _ = x.shape
    m = b * t
    tm = min(m, 1024)
    x2d = x.reshape(m, D_MODEL)
    pos_col = positions.reshape(m, 1).astype(jnp.int32)
    ret_cos, ret_sin, _ = _rope_tables(pos_col, _ret_rope_consts(), tm)
    nsa_tabs = _rope_tables(pos_col, _nsa_rope_consts(), tm)
    for i in range(DEPTH):
        j = i // 2
        if i % 2 == 0:
            x2d = _retention_layer(x2d, b, t, norm_mix[i], ret_w_in[j], ret_w_out[j], ret_cos, ret_sin, tm)
        else:
            x2d = _nsa_layer(x2d, b, t, norm_mix[i], nsa_w_in[j], nsa_cmp_pos[j], nsa_cmp_w1[j],
                             nsa_cmp_w2[j], nsa_w_out[j], nsa_tabs, tm)
        x2d = _ffn_layer(x2d, norm_ffn[i], ffn_w_gu[i], ffn_w_down[i], tm,
                         final_gain=norm_final if i == DEPTH - 1 else None)
    return x2d.reshape(b, t, D_MODEL)
```

```python
import functools

import numpy as np
import jax
import jax.numpy as jnp
from jax import lax
from jax.experimental import pallas as pl
from jax.experimental.pallas import tpu as pltpu

F32 = jnp.float32
BF16 = jnp.bfloat16

D_MODEL = 1024
DEPTH = 4
NORM_EPS = 1e-6
NEG = -1e30

RET_HEADS = 4
RET_QK_DIM = 256
RET_V_DIM = 512
RET_ROPE_THETA = 10000.0
RET_CHUNK = 256

NSA_HEADS = 16
NSA_KV_HEADS = 4
NSA_HEAD_DIM = 64
NSA_ROPE_THETA = 500000.0
NSA_ROPE_DIM = 16
CMP_BLOCK = 32
CMP_STRIDE = 16
CMP_HIDDEN = 256
SLC_BLOCK = 64
SLC_TOPK = 16
WINDOW = 512
FORCE = 1e4
NSA_QD = 1024
NSA_KVD = 256
FFN_HIDDEN = 2816

LANES = 128
SUBLANES = 8
MXU_DIM = 256
ROW_TILE = 512
VMEM_LIMIT = 56 * 1024 * 1024


def _nt_dot(a, b):
    return lax.dot_general(a, b, (((1,), (1,)), ((), ())), preferred_element_type=F32)


def _dot(a, b):
    return jnp.dot(a, b, preferred_element_type=F32)


def _rms_to_bf16(x, gain):
    ms = jnp.mean(x * x, axis=-1, keepdims=True)
    return (x * lax.rsqrt(ms + NORM_EPS) * gain).astype(BF16)


def _resident(shape, index_map):
    return pl.BlockSpec(shape, index_map, pipeline_mode=pl.Buffered(1))


def _rope_tables_kernel(pos_ref, c_ref, cos_ref, sa_ref, sb_ref):
    ang = pos_ref[...].astype(F32) * c_ref[0:1, :]
    s = jnp.sin(ang)
    cos_ref[...] = jnp.cos(ang)
    sa_ref[...] = s * c_ref[1:2, :]
    sb_ref[...] = s * c_ref[2:3, :]


def _rope_tables(pos_col, consts, tm):
    m = pos_col.shape[0]
    out = jax.ShapeDtypeStruct((m, LANES), F32)
    return pl.pallas_call(
        _rope_tables_kernel,
        grid=(m // tm,),
        in_specs=[pl.BlockSpec((tm, 1), lambda i: (i, 0)),
                  pl.BlockSpec((8, LANES), lambda i: (0, 0))],
        out_specs=[pl.BlockSpec((tm, LANES), lambda i: (i, 0))] * 3,
        out_shape=[out, out, out],
        name="rope_tables",
    )(pos_col, consts)


def _ret_proj_kernel(x_ref, gain_ref, wqk_ref, wv_ref, wg_ref, cos_ref, sin_ref, qk_ref, v_ref, g_ref):
    hn = _rms_to_bf16(x_ref[...], gain_ref[...])
    qk = _dot(hn, wqk_ref[...])
    c = cos_ref[...]
    s = sin_ref[...]
    for head in range(2 * RET_HEADS):
        scale = 1.0 if head < RET_HEADS else RET_QK_DIM ** -0.5
        lo = head * RET_QK_DIM
        x1 = qk[:, lo:lo + LANES]
        x2 = qk[:, lo + LANES:lo + 2 * LANES]
        qk_ref[:, lo:lo + LANES] = ((x1 * c - x2 * s) * scale).astype(BF16)
        qk_ref[:, lo + LANES:lo + 2 * LANES] = ((x2 * c + x1 * s) * scale).astype(BF16)
    v_ref[...] = _dot(hn, wv_ref[...]).astype(BF16)
    gate = _dot(hn, wg_ref[...])
    g_ref[...] = (gate * jax.nn.sigmoid(gate)).astype(BF16)


def _ret_proj(x2d, gain, w_bf16, cos, sin, tm):
    m = x2d.shape[0]
    width = 2 * D_MODEL
    out = jax.ShapeDtypeStruct((m, width), BF16)
    return pl.pallas_call(
        _ret_proj_kernel,
        grid=(m // tm,),
        in_specs=[pl.BlockSpec((tm, D_MODEL), lambda i: (i, 0)),
                  _resident((1, D_MODEL), lambda i: (0, 0)),
                  _resident((D_MODEL, width), lambda i: (0, 0)),
                  _resident((D_MODEL, width), lambda i: (0, 1)),
                  _resident((D_MODEL, width), lambda i: (0, 2)),
                  pl.BlockSpec((tm, LANES), lambda i: (i, 0)),
                  pl.BlockSpec((tm, LANES), lambda i: (i, 0))],
        out_specs=[pl.BlockSpec((tm, width), lambda i: (i, 0))] * 3,
        out_shape=[out, out, out],
        compiler_params=pltpu.CompilerParams(dimension_semantics=("arbitrary",),
                                             vmem_limit_bytes=VMEM_LIMIT),
        name="ret_proj",
    )(x2d, gain, w_bf16, w_bf16, w_bf16, cos, sin)


def _retention_kernel(q_ref, k_ref, v_ref, g_ref, o_ref, state, intra, qd, kd, *, chunk, n_chunks):
    c_len = chunk
    h = pl.program_id(1)
    t = pl.program_id(2)

    @pl.when(t == 0)
    def _():
        hv = jnp.full((c_len, RET_QK_DIM), h, jnp.int32)
        den = jnp.left_shift(jnp.full((c_len, RET_QK_DIM), 32, jnp.int32), hv).astype(F32)
        lg = jnp.log(1.0 - 1.0 / den)
        ri = lax.broadcasted_iota(jnp.int32, (c_len, RET_QK_DIM), 0)
        ci = lax.broadcasted_iota(jnp.int32, (c_len, RET_QK_DIM), 1)
        diff = (ri - ci).astype(F32)
        intra[...] = jnp.where(diff >= 0, jnp.exp(lg * jnp.maximum(diff, 0.0)), 0.0)
        rf = ri.astype(F32)
        qd[...] = jnp.exp(lg * (rf + 1.0))
        kd[...] = jnp.exp(lg * (c_len - 1.0 - rf))
        state[...] = jnp.zeros_like(state)

    chunk_decay = qd[c_len - 1:c_len, 0:1]
    for c in range(n_chunks):
        sl = pl.ds(c * c_len, c_len)
        qc = q_ref[0, sl, :]
        kc = k_ref[0, sl, :]
        vc = v_ref[0, sl, :]
        s = _nt_dot(qc, kc) * intra[...]
        st = state[...]
        qdec = (qc.astype(F32) * qd[...]).astype(BF16)
        o = _dot(s.astype(BF16), vc) + _dot(qdec, st.astype(BF16))
        kdec = (kc.astype(F32) * kd[...]).astype(BF16)
        state[...] = st * chunk_decay + lax.dot_general(
            kdec, vc, (((0,), (0,)), ((), ())), preferred_element_type=F32)
        ms = jnp.mean(o * o, axis=-1, keepdims=True)
        y = o * lax.rsqrt(ms + NORM_EPS)
        o_ref[0, sl, :] = (y * g_ref[0, sl, :].astype(F32)).astype(BF16)


def _retention(qk3, v3, g3, tq):
    b, t, _ = qk3.shape
    chunk = RET_CHUNK
    assert RET_CHUNK == RET_QK_DIM and tq % chunk == 0 and t % tq == 0
    kern = functools.partial(_retention_kernel, chunk=chunk, n_chunks=tq // chunk)
    nq = D_MODEL // RET_QK_DIM
    return pl.pallas_call(
        kern,
        grid=(b, RET_HEADS, t // tq),
        in_specs=[pl.BlockSpec((1, tq, RET_QK_DIM), lambda bb, h, tt: (bb, tt, h)),
                  pl.BlockSpec((1, tq, RET_QK_DIM), lambda bb, h, tt: (bb, tt, nq + h)),
                  pl.BlockSpec((1, tq, RET_V_DIM), lambda bb, h, tt: (bb, tt, h)),
                  pl.BlockSpec((1, tq, RET_V_DIM), lambda bb, h, tt: (bb, tt, h))],
        out_specs=pl.BlockSpec((1, tq, RET_V_DIM), lambda bb, h, tt: (bb, tt, h)),
        out_shape=jax.ShapeDtypeStruct((b, t, RET_HEADS * RET_V_DIM), BF16),
        scratch_shapes=[pltpu.VMEM((RET_QK_DIM, RET_V_DIM), F32),
                        pltpu.VMEM((chunk, RET_QK_DIM), F32),
                        pltpu.VMEM((chunk, RET_QK_DIM), F32),
                        pltpu.VMEM((chunk, RET_QK_DIM), F32)],
        compiler_params=pltpu.CompilerParams(dimension_semantics=("arbitrary",) * 3),
        name="retention",
    )(qk3, qk3, v3, g3)


def _matmul_res_kernel(a_ref, w_ref, res_ref, o_ref):
    o_ref[...] = res_ref[...] + _dot(a_ref[...], w_ref[...])


def _matmul_res(a, w_bf16, res, tm):
    m, k = a.shape
    n = w_bf16.shape[1]
    return pl.pallas_call(
        _matmul_res_kernel,
        grid=(m // tm,),
        in_specs=[pl.BlockSpec((tm, k), lambda i: (i, 0)),
                  _resident((k, n), lambda i: (0, 0)),
                  pl.BlockSpec((tm, n), lambda i: (i, 0))],
        out_specs=pl.BlockSpec((tm, n), lambda i: (i, 0)),
        out_shape=jax.ShapeDtypeStruct((m, n), F32),
        compiler_params=pltpu.CompilerParams(dimension_semantics=("arbitrary",)),
        name="matmul_res",
    )(a, w_bf16, res)


def _ffn_kernel(x_ref, gain_ref, wg_ref, wu_ref, wd_ref, *rest, final_norm):
    if final_norm:
        fgain_ref, o_ref = rest
    else:
        (o_ref,) = rest
    x = x_ref[...]
    hn = _rms_to_bf16(x, gain_ref[...])
    a = _dot(hn, wg_ref[...])
    b = _dot(hn, wu_ref[...])
    hid = (a * jax.nn.sigmoid(a) * b).astype(BF16)
    y = x + _dot(hid, wd_ref[...])
    if final_norm:
        ms = jnp.mean(y * y, axis=-1, keepdims=True)
        y = y * lax.rsqrt(ms + NORM_EPS) * fgain_ref[...]
    o_ref[...] = y


def _ffn(x2d, gain, w_gu_bf16, w_down_bf16, tm, final_gain=None):
    m = x2d.shape[0]
    in_specs = [pl.BlockSpec((tm, D_MODEL), lambda i: (i, 0)),
                _resident((1, D_MODEL), lambda i: (0, 0)),
                _resident((D_MODEL, FFN_HIDDEN), lambda i: (0, 0)),
                _resident((D_MODEL, FFN_HIDDEN), lambda i: (0, 1)),
                _resident((FFN_HIDDEN, D_MODEL), lambda i: (0, 0))]
    args = [x2d, gain, w_gu_bf16, w_gu_bf16, w_down_bf16]
    if final_gain is not None:
        in_specs.append(_resident((1, D_MODEL), lambda i: (0, 0)))
        args.append(final_gain)
    return pl.pallas_call(
        functools.partial(_ffn_kernel, final_norm=final_gain is not None),
        grid=(m // tm,),
        in_specs=in_specs,
        out_specs=pl.BlockSpec((tm, D_MODEL), lambda i: (i, 0)),
        out_shape=jax.ShapeDtypeStruct((m, D_MODEL), F32),
        compiler_params=pltpu.CompilerParams(dimension_semantics=("arbitrary",),
                                             vmem_limit_bytes=VMEM_LIMIT),
        name="ffn",
    )(*args)


NSA_QK = NSA_QD + 3 * NSA_KVD
NSA_V = 3 * NSA_KVD
NSA_GATE_PAD = NSA_KV_HEADS * LANES
Q_SCALE_LOG2 = NSA_HEAD_DIM ** -0.5 * float(np.log2(np.e))


def _partial_rotary(a, c, sa, sb):
    return a * c + pltpu.roll(a, 8, 1) * sa + pltpu.roll(a, LANES - 8, 1) * sb


def _nsa_proj_kernel(x_ref, gain_ref, wqk_ref, wv_ref, wg_ref, c_ref, sa_ref, sb_ref,
                     qk_ref, v_ref, gate_ref):
    hn = _rms_to_bf16(x_ref[...], gain_ref[...])
    qk = _dot(hn, wqk_ref[...])
    c = c_ref[...]
    sa = sa_ref[...]
    sb = sb_ref[...]
    for chunk in range(NSA_QK // LANES):
        scale = Q_SCALE_LOG2 if chunk < NSA_QD // LANES else 1.0
        sl = slice(chunk * LANES, (chunk + 1) * LANES)
        qk_ref[:, sl] = (_partial_rotary(qk[:, sl], c, sa, sb) * scale).astype(BF16)
    v_ref[...] = _dot(hn, wv_ref[...]).astype(BF16)
    gate_ref[...] = jax.nn.sigmoid(_dot(hn, wg_ref[...]))


def _nsa_proj(x2d, gain, wqk, wv, wg, c, sa, sb, tm):
    m = x2d.shape[0]
    row = lambda i: (i, 0)
    fixed = lambda i: (0, 0)
    return pl.pallas_call(
        _nsa_proj_kernel,
        grid=(m // tm,),
        in_specs=[pl.BlockSpec((tm, D_MODEL), row),
                  _resident((1, D_MODEL), fixed),
                  _resident((D_MODEL, NSA_QK), fixed),
                  _resident((D_MODEL, NSA_V), fixed),
                  _resident((D_MODEL, NSA_GATE_PAD), fixed),
                  pl.BlockSpec((tm, LANES), row),
                  pl.BlockSpec((tm, LANES), row),
                  pl.BlockSpec((tm, LANES), row)],
        out_specs=[pl.BlockSpec((tm, NSA_QK), row),
                   pl.BlockSpec((tm, NSA_V), row),
                   pl.BlockSpec((tm, NSA_GATE_PAD), row)],
        out_shape=[jax.ShapeDtypeStruct((m, NSA_QK), BF16),
                   jax.ShapeDtypeStruct((m, NSA_V), BF16),
                   jax.ShapeDtypeStruct((m, NSA_GATE_PAD), F32)],
        compiler_params=pltpu.CompilerParams(dimension_semantics=("arbitrary",),
                                             vmem_limit_bytes=VMEM_LIMIT),
        name="nsa_proj",
    )(x2d, gain, wqk, wv, wg, c, sa, sb)


def _compress_kernel(a_ref, pos_ref, w1_ref, w2_ref, o_ref):
    a = a_ref[0, 0, 0]
    nc = a.shape[0]
    half = CMP_STRIDE * NSA_HEAD_DIM
    upper = _dot(a, w1_ref[0, :half, :])
    lower = _dot(a, w1_ref[0, half:, :])
    cpos = _dot(pos_ref[0], w1_ref[0])
    hid = upper + pltpu.roll(lower, nc - 1, 0) + cpos[0:1, :]
    hid = hid * jax.nn.sigmoid(hid)
    o_ref[0, 0, 0] = _dot(hid.astype(BF16), w2_ref[0]).astype(BF16)


def _compress(a2, pos8, w1, w2):
    _, b, g, nc, width = a2.shape
    return pl.pallas_call(
        _compress_kernel,
        grid=(2, b, g),
        in_specs=[pl.BlockSpec((1, 1, 1, nc, width), lambda s, bb, gg: (s, bb, gg, 0, 0)),
                  pl.BlockSpec((1, 8, width * 2), lambda s, bb, gg: (s, 0, 0)),
                  pl.BlockSpec((1, width * 2, CMP_HIDDEN), lambda s, bb, gg: (s, 0, 0)),
                  pl.BlockSpec((1, CMP_HIDDEN, NSA_HEAD_DIM), lambda s, bb, gg: (s, 0, 0))],
        out_specs=pl.BlockSpec((1, 1, 1, nc, NSA_HEAD_DIM), lambda s, bb, gg: (s, bb, gg, 0, 0)),
        out_shape=jax.ShapeDtypeStruct((2, b, g, nc, NSA_HEAD_DIM), BF16),
        compiler_params=pltpu.CompilerParams(dimension_semantics=("arbitrary",) * 3),
        name="nsa_compress",
    )(a2, pos8, w1, w2)


def _t_cols(x):
    n = x.shape[1] // LANES
    return jnp.concatenate([x[:, c * LANES:(c + 1) * LANES].T for c in range(n)], axis=0)


def _t_rows(x):
    n = x.shape[0] // LANES
    return jnp.concatenate([x[c * LANES:(c + 1) * LANES, :].T for c in range(n)], axis=1)


def _flash_update(par, st, v_aug, m_sc, acc_sc):
    m_old = m_sc[par]
    m_new = jnp.maximum(m_old, jnp.max(st, axis=0, keepdims=True))
    p = jnp.exp2(st - m_new)
    acc_sc[par] = acc_sc[par] * jnp.exp2(m_old - m_new) + _dot(v_aug, p.astype(BF16))
    m_sc[par] = m_new


def _normalised(acc, par):
    half = LANES // 2
    return acc[:half] / acc[half:] if par == 0 else acc[half:] / acc[:half]


def _nsa_attn_kernel(q_ref, gate_ref, ovl_ref, kca_ref, kcb_ref, vc_ref,
                     ksa_ref, ksb_ref, vse_ref, vso_ref, kwa_ref, kwb_ref, vwe_ref, vwo_ref,
                     o_ref, qm_sc, qaug_sc, x_sc, cnt_sc, tot_sc, m_sc, acc_sc, sta_sc, stb_sc,
                     *, qt, kt, n_blocks, n_sel, n_back):
    qi = pl.program_id(2)
    t0 = qi * qt
    half = LANES // 2
    wide = 2 * qt
    lane = lax.broadcasted_iota(jnp.int32, (qt, LANES), 1)
    low = lane < NSA_HEAD_DIM
    zero = jnp.zeros((qt, LANES), BF16)
    qps = [q_ref[0, :, pair * LANES:(pair + 1) * LANES] for pair in range(2)]
    for pair in range(2):
        rows = slice(pair * qt, (pair + 1) * qt)
        qm_sc[0, rows] = jnp.where(low, qps[pair], zero)
        qm_sc[1, rows] = jnp.where(low, zero, qps[pair])

    gates_t = _t_rows(gate_ref[0])

    def gate_row(branch, par):
        return jnp.concatenate([gates_t[branch * 4 + 2 * pair + par:branch * 4 + 2 * pair + par + 1, :]
                                for pair in range(2)], axis=1)

    col = lax.broadcasted_iota(jnp.int32, (1, wide), 1)
    tcol = t0 + jnp.where(col >= qt, col - qt, col)

    nc = kca_ref.shape[2]
    kidx = lax.broadcasted_iota(jnp.int32, (nc, wide), 0)
    valid = kidx * CMP_STRIDE + (CMP_BLOCK - 1) <= tcol
    vct = vc_ref[0, 0]
    psum = None
    for par in range(2):
        st = _nt_dot((kcb_ref if par else kca_ref)[0, 0], qm_sc[par])
        st = jnp.where(valid, st, NEG)
        m = jnp.max(st, axis=0, keepdims=True)
        e = jnp.exp2(st - m)
        l = jnp.sum(e, axis=0, keepdims=True)
        p = jnp.where(valid, e / l, 0.0)
        ph = p[:, :qt] + p[:, qt:]
        psum = ph if psum is None else psum + ph
        o_t = _dot(vct, p.astype(BF16))
        tot_sc[par] = o_t[:half] * gate_row(0, par)

    ovl = ovl_ref[...]
    hi = psum.astype(BF16)
    r1 = psum - hi.astype(F32)
    mid = r1.astype(BF16)
    lo = (r1 - mid.astype(F32)).astype(BF16)
    imp = (_dot(ovl, hi) + _dot(ovl, mid) + _dot(ovl, lo))[:half, :]
    nio = lax.broadcasted_iota(jnp.int32, (half, qt), 0)
    cur = lax.shift_right_logical(t0 + lax.broadcasted_iota(jnp.int32, (half, qt), 1), 6)
    forced = (nio == 0) | (nio == cur) | (nio == cur - 1)
    x_sc[...] = jnp.where(nio > cur, NEG, jnp.where(forced, imp + FORCE, imp))
    cnt_sc[...] = jnp.zeros_like(cnt_sc)

    cur_max = lax.shift_right_logical(t0 + qt - 1, 6)
    sub = lax.broadcasted_iota(jnp.int32, (SUBLANES, qt), 0)
    for rm in range(-(-n_blocks // SUBLANES)):
        @pl.when(rm * SUBLANES <= cur_max)
        def _(rm=rm):
            rows = x_sc[rm * SUBLANES:(rm + 1) * SUBLANES, :]
            for r in range(half // SUBLANES):
                grp = slice(r * SUBLANES, (r + 1) * SUBLANES)
                xr = x_sc[grp, :]
                cnt = cnt_sc[grp, :]
                for mm in range(SUBLANES):
                    row = rows[mm:mm + 1, :]
                    if r < rm:
                        beats = row > xr
                    elif r > rm:
                        beats = row >= xr
                    else:
                        beats = (row > xr) | ((row == xr) & (sub > mm))
                    cnt = cnt + jnp.where(beats, 1, 0)
                cnt_sc[grp, :] = cnt
    bias_t = jnp.where(cnt_sc[...] < n_sel, 0.0, NEG).astype(F32)
    bias = _t_cols(jnp.concatenate([bias_t, bias_t], axis=0)).astype(BF16)
    for pair in range(2):
        rows = slice(pair * qt, (pair + 1) * qt)
        qaug_sc[0, rows] = jnp.where(low, qps[pair], bias)
        qaug_sc[1, rows] = jnp.where(low, bias, qps[pair])

    ri = lax.broadcasted_iota(jnp.int32, (qt, wide), 0)
    ci = lax.broadcasted_iota(jnp.int32, (qt, wide), 1)
    ci = jnp.where(ci >= qt, ci - qt, ci)
    keeps, widx = [], []
    for back in range(n_back + 1):
        present = qi >= back
        widx.append(jnp.maximum(qi - back, 0))
        if back == 0:
            keeps.append(ri <= ci)
        elif back == n_back:
            keeps.append((ri > ci) & present)
        else:
            keeps.append(jnp.broadcast_to(present, (qt, wide)))
    for par in range(2):
        k_ref = kwb_ref if par else kwa_ref
        v_ref = vwo_ref if par else vwe_ref
        sts = []
        for back in range(n_back + 1):
            ks = pl.multiple_of(widx[back] * qt, qt)
            st = _nt_dot(k_ref[0, 0, pl.ds(ks, qt), :], qm_sc[par])
            sts.append(jnp.where(keeps[back], st, NEG))
        m = functools.reduce(jnp.maximum, [jnp.max(st, axis=0, keepdims=True) for st in sts])
        acc = None
        for back in range(n_back + 1):
            part = _dot(v_ref[0, 0, widx[back]], jnp.exp2(sts[back] - m).astype(BF16))
            acc = part if acc is None else acc + part
        tot_sc[par] = tot_sc[par] + _normalised(acc, par) * gate_row(2, par)

    m_sc[...] = jnp.full(m_sc.shape, NEG, F32)
    acc_sc[...] = jnp.zeros_like(acc_sc)

    def scores(kidx, buf):
        ks = pl.multiple_of(kidx * kt, kt)
        buf[0] = _nt_dot(ksa_ref[0, 0, pl.ds(ks, kt), :], qaug_sc[0])
        buf[1] = _nt_dot(ksb_ref[0, 0, pl.ds(ks, kt), :], qaug_sc[1])

    def consume(kidx, buf, masked):
        v_aug = (vse_ref[0, 0, kidx], vso_ref[0, 0, kidx])
        if masked:
            keep = kidx * kt + lax.broadcasted_iota(jnp.int32, (kt, wide), 0) <= tcol
        for par in range(2):
            st = buf[par]
            if masked:
                st = jnp.where(keep, st, NEG)
            _flash_update(par, st, v_aug[par], m_sc, acc_sc)

    assert qt <= kt
    last = (t0 + qt - 1) // kt
    scores(0, sta_sc)

    def body(j, carry):
        scores(2 * j + 1, stb_sc)
        consume(2 * j, sta_sc, False)
        scores(2 * j + 2, sta_sc)
        consume(2 * j + 1, stb_sc, False)
        return carry

    lax.fori_loop(0, last // 2, body, 0)

    @pl.when(last % 2 == 0)
    def _():
        consume(last, sta_sc, True)

    @pl.when(last % 2 == 1)
    def _():
        scores(last, stb_sc)
        consume(last - 1, sta_sc, False)
        consume(last, stb_sc, True)

    total = [tot_sc[par] + _normalised(acc_sc[par], par) * gate_row(1, par) for par in range(2)]
    for pair in range(2):
        cols = slice(pair * qt, (pair + 1) * qt)
        pair_t = jnp.concatenate([total[0][:, cols], total[1][:, cols]], axis=0)
        o_ref[0, :, pair * LANES:(pair + 1) * LANES] = _t_cols(pair_t).astype(BF16)


def _nsa_attn(qk3, gates3, ovl, kca, kcb, vc2, ksa, ksb, vse, vso, kwa, kwb, vwe, vwo, qt, kt):
    b, t, _ = qk3.shape
    g = NSA_KV_HEADS
    nc = kca.shape[2]
    n_blocks = t // SLC_BLOCK
    assert n_blocks <= LANES // 2 and SLC_BLOCK == 64 and WINDOW % qt == 0
    kern = functools.partial(_nsa_attn_kernel, qt=qt, kt=kt, n_blocks=n_blocks,
                             n_sel=min(SLC_TOPK, n_blocks), n_back=WINDOW // qt)
    per_group = lambda *tail: (lambda bb, gg, tt: (bb, gg) + tail)
    kc_spec = pl.BlockSpec((1, 1, nc, LANES), per_group(0, 0))
    k_spec = pl.BlockSpec((1, 1, t, LANES), per_group(0, 0))
    vs_spec = pl.BlockSpec((1, 1, t // kt, LANES, kt), per_group(0, 0, 0))
    vw_spec = pl.BlockSpec((1, 1, t // qt, LANES, qt), per_group(0, 0, 0))
    half = LANES // 2
    return pl.pallas_call(
        kern,
        grid=(b, g, t // qt),
        in_specs=[pl.BlockSpec((1, qt, MXU_DIM), lambda bb, gg, tt: (bb, tt, gg)),
                  pl.BlockSpec((1, qt, LANES), lambda bb, gg, tt: (bb, tt, gg)),
                  pl.BlockSpec((LANES, nc), lambda bb, gg, tt: (0, 0)),
                  kc_spec, kc_spec,
                  pl.BlockSpec((1, 1, LANES, nc), per_group(0, 0)),
                  k_spec, k_spec, vs_spec, vs_spec,
                  k_spec, k_spec, vw_spec, vw_spec],
        out_specs=pl.BlockSpec((1, qt, MXU_DIM), lambda bb, gg, tt: (bb, tt, gg)),
        out_shape=jax.ShapeDtypeStruct((b, t, NSA_QD), BF16),
        scratch_shapes=[pltpu.VMEM((2, 2 * qt, LANES), BF16),
                        pltpu.VMEM((2, 2 * qt, LANES), BF16),
                        pltpu.VMEM((half, qt), F32),
                        pltpu.VMEM((half, qt), jnp.int32),
                        pltpu.VMEM((2, half, 2 * qt), F32),
                        pltpu.VMEM((2, 1, 2 * qt), F32),
                        pltpu.VMEM((2, LANES, 2 * qt), F32),
                        pltpu.VMEM((2, kt, 2 * qt), F32),
                        pltpu.VMEM((2, kt, 2 * qt), F32)],
        compiler_params=pltpu.CompilerParams(dimension_semantics=("arbitrary",) * 3,
                                             vmem_limit_bytes=VMEM_LIMIT),
        name="nsa_attn",
    )(qk3, gates3, ovl, kca, kcb, vc2, ksa, ksb, vse, vso, kwa, kwb, vwe, vwo)


def _ret_rope_consts():
    half = RET_QK_DIM // 2
    inv_freq = RET_ROPE_THETA ** (-2.0 * jnp.arange(half, dtype=F32) / RET_QK_DIM)
    rows = jnp.zeros((8, LANES), F32)
    return rows.at[0].set(inv_freq).at[1].set(1.0)


def _nsa_rope_consts():
    half = NSA_ROPE_DIM // 2
    inv_freq = NSA_ROPE_THETA ** (-2.0 * jnp.arange(half, dtype=F32) / NSA_ROPE_DIM)
    r = np.arange(LANES) % NSA_HEAD_DIM
    rot = r < NSA_ROPE_DIM
    freq = jnp.where(rot, inv_freq[r % half], 0.0)
    mask_a = ((r >= half) & rot).astype(np.float32)
    mask_b = -(r < half).astype(np.float32)
    rows = jnp.zeros((8, LANES), F32)
    return rows.at[0].set(freq).at[1].set(mask_a).at[2].set(mask_b)


def _nsa_weight_columns():
    kv0 = NSA_QD
    cols = list(range(NSA_QD))
    for kvsel in range(2):
        for branch in range(3):
            base = kv0 + (branch * 2 + kvsel) * NSA_KVD
            cols += list(range(base, base + NSA_KVD))
    gate0 = NSA_QD + 6 * NSA_KVD
    hg = NSA_HEADS // NSA_KV_HEADS
    for g in range(NSA_KV_HEADS):
        lanes = [-1] * LANES
        for branch in range(3):
            for hh in range(hg):
                lanes[branch * 4 + hh] = gate0 + (g * hg + hh) * 3 + branch
        cols += lanes
    return np.asarray(cols, np.int32)


def _group_major(a, b, t):
    return a.reshape(b, t, NSA_KV_HEADS, NSA_HEAD_DIM).transpose(0, 2, 1, 3)


def _pad_pair(k):
    z = jnp.zeros_like(k)
    return jnp.concatenate([k, z], axis=-1), jnp.concatenate([z, k], axis=-1)


def _value_tiles(v, kt):
    b, g, t, d = v.shape
    vt = v.reshape(b, g, t // kt, kt, d).transpose(0, 1, 2, 4, 3)
    ones = jnp.ones_like(vt)
    return jnp.concatenate([vt, ones], axis=3), jnp.concatenate([ones, vt], axis=3)


def _retention_layer(x2d, b, t, gain, w_in, w_out, cos, sin, tm):
    qk, v, gate = _ret_proj(x2d, gain.reshape(1, D_MODEL), w_in.astype(BF16), cos, sin, tm)
    wide = lambda a: a.reshape(b, t, 2 * D_MODEL)
    o = _retention(wide(qk), wide(v), wide(gate), min(t, 1024))
    return _matmul_res(o.reshape(b * t, RET_HEADS * RET_V_DIM), w_out.astype(BF16), x2d, tm)


def _nsa_layer(x2d, b, t, gain, w_in, cmp_pos, cmp_w1, cmp_w2, w_out, tabs, tm):
    g = NSA_KV_HEADS
    cols = _nsa_weight_columns()
    w_ext = jnp.concatenate([w_in, jnp.zeros((D_MODEL, 1), w_in.dtype)], axis=1)
    w_re = jnp.take(w_ext, jnp.asarray(np.where(cols < 0, w_in.shape[1], cols)), axis=1).astype(BF16)
    main, vals, gates = _nsa_proj(x2d, gain.reshape(1, D_MODEL), w_re[:, :NSA_QK],
                                  w_re[:, NSA_QK:NSA_QK + NSA_V], w_re[:, NSA_QK + NSA_V:], *tabs, tm)

    def group_major(a, i):
        return _group_major(a[:, i * NSA_KVD:(i + 1) * NSA_KVD], b, t)

    k_cmp_tok, k_slc, k_win = (group_major(main[:, NSA_QD:], i) for i in range(3))
    v_cmp_tok, v_slc, v_win = (group_major(vals, i) for i in range(3))

    nc = t // CMP_STRIDE
    a2 = jnp.stack([k_cmp_tok, v_cmp_tok]).reshape(2, b, g, nc, CMP_STRIDE * NSA_HEAD_DIM)
    pos8 = jnp.broadcast_to(cmp_pos.reshape(2, 1, CMP_BLOCK * NSA_HEAD_DIM),
                            (2, 8, CMP_BLOCK * NSA_HEAD_DIM)).astype(BF16)
    cmp = _compress(a2, pos8, cmp_w1.astype(BF16), cmp_w2.astype(BF16))
    kca, kcb = _pad_pair(cmp[0])
    vct = cmp[1].transpose(0, 1, 3, 2)
    vc2 = jnp.concatenate([vct, vct], axis=2)

    n_blocks = t // SLC_BLOCK
    kk = np.arange(nc)[None, :]
    nn = np.arange(LANES)[:, None]
    ovl = ((kk * CMP_STRIDE < nn * SLC_BLOCK + SLC_BLOCK) & (kk * CMP_STRIDE + CMP_BLOCK > nn * SLC_BLOCK)
           & (nn < n_blocks) & (kk < nc - 1))
    ovl = jnp.asarray(ovl, BF16)

    onehot = jnp.asarray(np.arange(t)[:, None] // SLC_BLOCK == np.arange(NSA_HEAD_DIM)[None, :], BF16)
    onehot = jnp.broadcast_to(onehot, (b, g, t, NSA_HEAD_DIM))
    ksa = jnp.concatenate([k_slc, onehot], axis=-1)
    ksb = jnp.concatenate([onehot, k_slc], axis=-1)
    kwa, kwb = _pad_pair(k_win)
    qt = min(t, 256)
    kt = min(t, 512)
    attn = _nsa_attn(main.reshape(b, t, NSA_QK), gates.reshape(b, t, NSA_GATE_PAD), ovl, kca, kcb, vc2,
                     ksa, ksb, *_value_tiles(v_slc, kt), kwa, kwb, *_value_tiles(v_win, qt), qt, kt)
    return _matmul_res(attn.reshape(b * t, NSA_QD), w_out.astype(BF16), x2d, tm)


def _ffn_layer(x2d, gain, w_gu, w_down, tm, final_gain=None):
    fg = None if final_gain is None else final_gain.reshape(1, D_MODEL)
    return _ffn(x2d, gain.reshape(1, D_MODEL), w_gu.astype(BF16), w_down.astype(BF16), tm, final_gain=fg)


def kernel(x, positions, norm_mix, norm_ffn, norm_final, ret_w_in, ret_w_out, nsa_w_in, nsa_cmp_pos,
           nsa_cmp_w1, nsa_cmp_w2, nsa_w_out, ffn_w_gu, ffn_w_down):
    b, t, _ = x.shape
    m = b * t
    tm = min(m, ROW_TILE)
    x2d = x.reshape(m, D_MODEL)
    pos_col = positions.reshape(m, 1).astype(jnp.int32)
    ret_cos, ret_sin, _ = _rope_tables(pos_col, _ret_rope_consts(), tm)
    nsa_tabs = _rope_tables(pos_col, _nsa_rope_consts(), tm)
    for i in range(DEPTH):
        j = i // 2
        if i % 2 == 0:
            x2d = _retention_layer(x2d, b, t, norm_mix[i], ret_w_in[j], ret_w_out[j], ret_cos, ret_sin, tm)
        else:
            x2d = _nsa_layer(x2d, b, t, norm_mix[i], nsa_w_in[j], nsa_cmp_pos[j], nsa_cmp_w1[j],
                             nsa_cmp_w2[j], nsa_w_out[j], nsa_tabs, tm)
        x2d = _ffn_layer(x2d, norm_ffn[i], ffn_w_gu[i], ffn_w_down[i], tm,
                         final_gain=norm_final if i == DEPTH - 1 else None)
    return x2d.reshape(b, t, D_MODEL)
```

```python
import functools

import numpy as np
import jax
import jax.numpy as jnp
from jax import lax
from jax.experimental import pallas as pl
from jax.experimental.pallas import tpu as pltpu

F32 = jnp.float32
BF16 = jnp.bfloat16

D_MODEL = 1024
DEPTH = 4
NORM_EPS = 1e-6
NEG = -1e30

RET_HEADS = 4
RET_QK_DIM = 256
RET_V_DIM = 512
RET_ROPE_THETA = 10000.0
RET_CHUNK = 256

NSA_HEADS = 16
NSA_KV_HEADS = 4
NSA_HEAD_DIM = 64
NSA_ROPE_THETA = 500000.0
NSA_ROPE_DIM = 16
CMP_BLOCK = 32
CMP_STRIDE = 16
CMP_HIDDEN = 256
SLC_BLOCK = 64
SLC_TOPK = 16
WINDOW = 512
FORCE = 1e4
NSA_QD = 1024
NSA_KVD = 256
FFN_HIDDEN = 2816

LANES = 128
SUBLANES = 8
MXU_DIM = 256
ROW_TILE = 512
VMEM_LIMIT = 56 * 1024 * 1024


def _nt_dot(a, b):
    return lax.dot_general(a, b, (((1,), (1,)), ((), ())), preferred_element_type=F32)


def _dot(a, b):
    return jnp.dot(a, b, preferred_element_type=F32)


def _rms_to_bf16(x, gain):
    ms = jnp.mean(x * x, axis=-1, keepdims=True)
    return (x * lax.rsqrt(ms + NORM_EPS) * gain).astype(BF16)


def _resident(shape, index_map):
    return pl.BlockSpec(shape, index_map, pipeline_mode=pl.Buffered(1))


def _rope_tables_kernel(pos_ref, c_ref, cos_ref, sa_ref, sb_ref):
    ang = pos_ref[...].astype(F32) * c_ref[0:1, :]
    s = jnp.sin(ang)
    cos_ref[...] = jnp.cos(ang)
    sa_ref[...] = s * c_ref[1:2, :]
    sb_ref[...] = s * c_ref[2:3, :]


def _rope_tables(pos_col, consts, tm):
    m = pos_col.shape[0]
    out = jax.ShapeDtypeStruct((m, LANES), F32)
    return pl.pallas_call(
        _rope_tables_kernel,
        grid=(m // tm,),
        in_specs=[pl.BlockSpec((tm, 1), lambda i: (i, 0)),
                  pl.BlockSpec((8, LANES), lambda i: (0, 0))],
        out_specs=[pl.BlockSpec((tm, LANES), lambda i: (i, 0))] * 3,
        out_shape=[out, out, out],
        name="rope_tables",
    )(pos_col, consts)


def _ret_proj_kernel(x_ref, gain_ref, wqk_ref, wv_ref, wg_ref, cos_ref, sin_ref, qk_ref, v_ref, g_ref):
    hn = _rms_to_bf16(x_ref[...], gain_ref[...])
    qk = _dot(hn, wqk_ref[...])
    c = cos_ref[...]
    s = sin_ref[...]
    for head in range(2 * RET_HEADS):
        scale = 1.0 if head < RET_HEADS else RET_QK_DIM ** -0.5
        lo = head * RET_QK_DIM
        x1 = qk[:, lo:lo + LANES]
        x2 = qk[:, lo + LANES:lo + 2 * LANES]
        qk_ref[:, lo:lo + LANES] = ((x1 * c - x2 * s) * scale).astype(BF16)
        qk_ref[:, lo + LANES:lo + 2 * LANES] = ((x2 * c + x1 * s) * scale).astype(BF16)
    v_ref[...] = _dot(hn, wv_ref[...]).astype(BF16)
    gate = _dot(hn, wg_ref[...])
    g_ref[...] = (gate * jax.nn.sigmoid(gate)).astype(BF16)


def _ret_proj(x2d, gain, w_bf16, cos, sin, tm):
    m = x2d.shape[0]
    width = 2 * D_MODEL
    out = jax.ShapeDtypeStruct((m, width), BF16)
    return pl.pallas_call(
        _ret_proj_kernel,
        grid=(m // tm,),
        in_specs=[pl.BlockSpec((tm, D_MODEL), lambda i: (i, 0)),
                  _resident((1, D_MODEL), lambda i: (0, 0)),
                  _resident((D_MODEL, width), lambda i: (0, 0)),
                  _resident((D_MODEL, width), lambda i: (0, 1)),
                  _resident((D_MODEL, width), lambda i: (0, 2)),
                  pl.BlockSpec((tm, LANES), lambda i: (i, 0)),
                  pl.BlockSpec((tm, LANES), lambda i: (i, 0))],
        out_specs=[pl.BlockSpec((tm, width), lambda i: (i, 0))] * 3,
        out_shape=[out, out, out],
        compiler_params=pltpu.CompilerParams(dimension_semantics=("arbitrary",),
                                             vmem_limit_bytes=VMEM_LIMIT),
        name="ret_proj",
    )(x2d, gain, w_bf16, w_bf16, w_bf16, cos, sin)


def _retention_kernel(q_ref, k_ref, v_ref, g_ref, o_ref, state, intra, qd, kd, *, chunk, n_chunks):
    c_len = chunk
    h = pl.program_id(1)
    t = pl.program_id(2)

    @pl.when(t == 0)
    def _():
        hv = jnp.full((c_len, RET_QK_DIM), h, jnp.int32)
        den = jnp.left_shift(jnp.full((c_len, RET_QK_DIM), 32, jnp.int32), hv).astype(F32)
        lg = jnp.log(1.0 - 1.0 / den)
        ri = lax.broadcasted_iota(jnp.int32, (c_len, RET_QK_DIM), 0)
        ci = lax.broadcasted_iota(jnp.int32, (c_len, RET_QK_DIM), 1)
        diff = (ri - ci).astype(F32)
        intra[...] = jnp.where(diff >= 0, jnp.exp(lg * jnp.maximum(diff, 0.0)), 0.0)
        rf = ri.astype(F32)
        qd[...] = jnp.exp(lg * (rf + 1.0))
        kd[...] = jnp.exp(lg * (c_len - 1.0 - rf))
        state[...] = jnp.zeros_like(state)

    chunk_decay = qd[c_len - 1:c_len, 0:1]
    for c in range(n_chunks):
        sl = pl.ds(c * c_len, c_len)
        qc = q_ref[0, sl, :]
        kc = k_ref[0, sl, :]
        vc = v_ref[0, sl, :]
        s = _nt_dot(qc, kc) * intra[...]
        st = state[...]
        qdec = (qc.astype(F32) * qd[...]).astype(BF16)
        o = _dot(s.astype(BF16), vc) + _dot(qdec, st.astype(BF16))
        kdec = (kc.astype(F32) * kd[...]).astype(BF16)
        state[...] = st * chunk_decay + lax.dot_general(
            kdec, vc, (((0,), (0,)), ((), ())), preferred_element_type=F32)
        ms = jnp.mean(o * o, axis=-1, keepdims=True)
        y = o * lax.rsqrt(ms + NORM_EPS)
        o_ref[0, sl, :] = (y * g_ref[0, sl, :].astype(F32)).astype(BF16)


def _retention(qk3, v3, g3, tq):
    b, t, _ = qk3.shape
    chunk = RET_CHUNK
    assert RET_CHUNK == RET_QK_DIM and tq % chunk == 0 and t % tq == 0
    kern = functools.partial(_retention_kernel, chunk=chunk, n_chunks=tq // chunk)
    nq = D_MODEL // RET_QK_DIM
    return pl.pallas_call(
        kern,
        grid=(b, RET_HEADS, t // tq),
        in_specs=[pl.BlockSpec((1, tq, RET_QK_DIM), lambda bb, h, tt: (bb, tt, h)),
                  pl.BlockSpec((1, tq, RET_QK_DIM), lambda bb, h, tt: (bb, tt, nq + h)),
                  pl.BlockSpec((1, tq, RET_V_DIM), lambda bb, h, tt: (bb, tt, h)),
                  pl.BlockSpec((1, tq, RET_V_DIM), lambda bb, h, tt: (bb, tt, h))],
        out_specs=pl.BlockSpec((1, tq, RET_V_DIM), lambda bb, h, tt: (bb, tt, h)),
        out_shape=jax.ShapeDtypeStruct((b, t, RET_HEADS * RET_V_DIM), BF16),
        scratch_shapes=[pltpu.VMEM((RET_QK_DIM, RET_V_DIM), F32),
                        pltpu.VMEM((chunk, RET_QK_DIM), F32),
                        pltpu.VMEM((chunk, RET_QK_DIM), F32),
                        pltpu.VMEM((chunk, RET_QK_DIM), F32)],
        compiler_params=pltpu.CompilerParams(dimension_semantics=("arbitrary",) * 3),
        name="retention",
    )(qk3, qk3, v3, g3)


def _matmul_res_kernel(a_ref, w_ref, res_ref, o_ref):
    o_ref[...] = res_ref[...] + _dot(a_ref[...], w_ref[...])


def _matmul_res(a, w_bf16, res, tm):
    m, k = a.shape
    n = w_bf16.shape[1]
    return pl.pallas_call(
        _matmul_res_kernel,
        grid=(m // tm,),
        in_specs=[pl.BlockSpec((tm, k), lambda i: (i, 0)),
                  _resident((k, n), lambda i: (0, 0)),
                  pl.BlockSpec((tm, n), lambda i: (i, 0))],
        out_specs=pl.BlockSpec((tm, n), lambda i: (i, 0)),
        out_shape=jax.ShapeDtypeStruct((m, n), F32),
        compiler_params=pltpu.CompilerParams(dimension_semantics=("arbitrary",)),
        name="matmul_res",
    )(a, w_bf16, res)


def _ffn_kernel(x_ref, gain_ref, wg_ref, wu_ref, wd_ref, *rest, final_norm):
    if final_norm:
        fgain_ref, o_ref = rest
    else:
        (o_ref,) = rest
    x = x_ref[...]
    hn = _rms_to_bf16(x, gain_ref[...])
    a = _dot(hn, wg_ref[...])
    b = _dot(hn, wu_ref[...])
    hid = (a * jax.nn.sigmoid(a) * b).astype(BF16)
    y = x + _dot(hid, wd_ref[...])
    if final_norm:
        ms = jnp.mean(y * y, axis=-1, keepdims=True)
        y = y * lax.rsqrt(ms + NORM_EPS) * fgain_ref[...]
    o_ref[...] = y


def _ffn(x2d, gain, w_gu_bf16, w_down_bf16, tm, final_gain=None):
    m = x2d.shape[0]
    in_specs = [pl.BlockSpec((tm, D_MODEL), lambda i: (i, 0)),
                _resident((1, D_MODEL), lambda i: (0, 0)),
                _resident((D_MODEL, FFN_HIDDEN), lambda i: (0, 0)),
                _resident((D_MODEL, FFN_HIDDEN), lambda i: (0, 1)),
                _resident((FFN_HIDDEN, D_MODEL), lambda i: (0, 0))]
    args = [x2d, gain, w_gu_bf16, w_gu_bf16, w_down_bf16]
    if final_gain is not None:
        in_specs.append(_resident((1, D_MODEL), lambda i: (0, 0)))
        args.append(final_gain)
    return pl.pallas_call(
        functools.partial(_ffn_kernel, final_norm=final_gain is not None),
        grid=(m // tm,),
        in_specs=in_specs,
        out_specs=pl.BlockSpec((tm, D_MODEL), lambda i: (i, 0)),
        out_shape=jax.ShapeDtypeStruct((m, D_MODEL), F32),
        compiler_params=pltpu.CompilerParams(dimension_semantics=("arbitrary",),
                                             vmem_limit_bytes=VMEM_LIMIT),
        name="ffn",
    )(*args)


NSA_QK = NSA_QD + 3 * NSA_KVD
NSA_V = 3 * NSA_KVD
NSA_GATE_PAD = NSA_KV_HEADS * LANES
Q_SCALE_LOG2 = NSA_HEAD_DIM ** -0.5 * float(np.log2(np.e))


def _partial_rotary(a, c, sa, sb):
    return a * c + pltpu.roll(a, 8, 1) * sa + pltpu.roll(a, LANES - 8, 1) * sb


def _t_rows(x):
    n = x.shape[0] // LANES
    return jnp.concatenate([x[c * LANES:(c + 1) * LANES, :].T for c in range(n)], axis=1)


def _nsa_proj_kernel(x_ref, gain_ref, wqk_ref, wv_ref, wg_ref, c_ref, sa_ref, sb_ref,
                     q_ref, kc_ref, vc_ref, gate_ref, ksa_ref, ksb_ref, kwa_ref, kwb_ref,
                     vse_ref, vso_ref, vwe_ref, vwo_ref, *, t, kt, qt):
    tm = x_ref.shape[0]
    half = LANES // 2
    hn = _rms_to_bf16(x_ref[...], gain_ref[...])
    qk = _dot(hn, wqk_ref[...])
    c = c_ref[...]
    sa = sa_ref[...]
    sb = sb_ref[...]

    def rotated(lo, scale=1.0):
        return _partial_rotary(qk[:, lo:lo + LANES], c, sa, sb) * scale

    for chunk in range(NSA_QD // LANES):
        q_ref[:, chunk * LANES:(chunk + 1) * LANES] = rotated(chunk * LANES, Q_SCALE_LOG2).astype(BF16)
    for chunk in range(NSA_KVD // LANES):
        kc_ref[:, chunk * LANES:(chunk + 1) * LANES] = rotated(NSA_QD + chunk * LANES).astype(BF16)

    lane = lax.broadcasted_iota(jnp.int32, (tm, LANES), 1)
    low = lane < half
    pos = lax.rem(pl.program_id(0) * tm + lax.broadcasted_iota(jnp.int32, (tm, LANES), 0), t)
    onehot = jnp.where((lane & (half - 1)) == lax.shift_right_logical(pos, 6), 1.0, 0.0)
    zero = jnp.zeros((tm, LANES), F32)
    for branch, (a_ref, b_ref, fill) in enumerate(((ksa_ref, ksb_ref, onehot), (kwa_ref, kwb_ref, zero))):
        base = NSA_QD + (1 + branch) * NSA_KVD
        for chunk in range(NSA_KVD // LANES):
            kk = rotated(base + chunk * LANES)
            swapped = pltpu.roll(kk, half, 1)
            a_ref[2 * chunk] = jnp.where(low, kk, fill).astype(BF16)
            b_ref[2 * chunk] = jnp.where(low, fill, swapped).astype(BF16)
            a_ref[2 * chunk + 1] = jnp.where(low, swapped, fill).astype(BF16)
            b_ref[2 * chunk + 1] = jnp.where(low, fill, kk).astype(BF16)

    v = _dot(hn, wv_ref[...])
    vc_ref[...] = v[:, :NSA_KVD].astype(BF16)
    ones = jnp.ones((half, tm), F32)
    for branch, (e_ref, o_ref, tile) in enumerate(((vse_ref, vso_ref, kt), (vwe_ref, vwo_ref, qt))):
        base = (1 + branch) * NSA_KVD
        for chunk in range(NSA_KVD // LANES):
            v_t = _t_rows(v[:, base + chunk * LANES:base + (chunk + 1) * LANES])
            for par in range(2):
                vg = v_t[par * half:(par + 1) * half]
                even = jnp.concatenate([vg, ones], axis=0).astype(BF16)
                odd = jnp.concatenate([ones, vg], axis=0).astype(BF16)
                for j in range(tm // tile):
                    e_ref[2 * chunk + par, j] = even[:, j * tile:(j + 1) * tile]
                    o_ref[2 * chunk + par, j] = odd[:, j * tile:(j + 1) * tile]
    gate_ref[...] = jax.nn.sigmoid(_dot(hn, wg_ref[...]))


def _nsa_proj(x2d, gain, wqk, wv, wg, c, sa, sb, tm, t, kt, qt):
    m = x2d.shape[0]
    g = NSA_KV_HEADS
    assert SLC_BLOCK == 64 and t % tm == 0 and tm % kt == 0 and tm % qt == 0
    row = lambda i: (i, 0)
    fixed = lambda i: (0, 0)
    key_spec = pl.BlockSpec((g, tm, LANES), lambda i: (0, i, 0))
    key_shape = jax.ShapeDtypeStruct((g, m, LANES), BF16)

    def val(tile):
        return (pl.BlockSpec((g, tm // tile, LANES, tile), lambda i: (0, i, 0, 0)),
                jax.ShapeDtypeStruct((g, m // tile, LANES, tile), BF16))

    return pl.pallas_call(
        functools.partial(_nsa_proj_kernel, t=t, kt=kt, qt=qt),
        grid=(m // tm,),
        in_specs=[pl.BlockSpec((tm, D_MODEL), row),
                  _resident((1, D_MODEL), fixed),
                  _resident((D_MODEL, NSA_QK), fixed),
                  _resident((D_MODEL, NSA_V), fixed),
                  _resident((D_MODEL, NSA_GATE_PAD), fixed),
                  pl.BlockSpec((tm, LANES), row),
                  pl.BlockSpec((tm, LANES), row),
                  pl.BlockSpec((tm, LANES), row)],
        out_specs=[pl.BlockSpec((tm, NSA_QD), row),
                   pl.BlockSpec((tm, NSA_KVD), row),
                   pl.BlockSpec((tm, NSA_KVD), row),
                   pl.BlockSpec((tm, NSA_GATE_PAD), row),
                   key_spec, key_spec, key_spec, key_spec,
                   val(kt)[0], val(kt)[0], val(qt)[0], val(qt)[0]],
        out_shape=[jax.ShapeDtypeStruct((m, NSA_QD), BF16),
                   jax.ShapeDtypeStruct((m, NSA_KVD), BF16),
                   jax.ShapeDtypeStruct((m, NSA_KVD), BF16),
                   jax.ShapeDtypeStruct((m, NSA_GATE_PAD), F32),
                   key_shape, key_shape, key_shape, key_shape,
                   val(kt)[1], val(kt)[1], val(qt)[1], val(qt)[1]],
        compiler_params=pltpu.CompilerParams(dimension_semantics=("arbitrary",),
                                             vmem_limit_bytes=VMEM_LIMIT),
        name="nsa_proj",
    )(x2d, gain, wqk, wv, wg, c, sa, sb)


def _compress_kernel(a_ref, pos_ref, w1_ref, w2_ref, o_ref):
    a = a_ref[0, 0, 0]
    nc = a.shape[0]
    half = CMP_STRIDE * NSA_HEAD_DIM
    upper = _dot(a, w1_ref[0, :half, :])
    lower = _dot(a, w1_ref[0, half:, :])
    cpos = _dot(pos_ref[0], w1_ref[0])
    hid = upper + pltpu.roll(lower, nc - 1, 0) + cpos[0:1, :]
    hid = hid * jax.nn.sigmoid(hid)
    o_ref[0, 0, 0] = _dot(hid.astype(BF16), w2_ref[0]).astype(BF16)


def _compress(a2, pos8, w1, w2):
    _, b, g, nc, width = a2.shape
    return pl.pallas_call(
        _compress_kernel,
        grid=(2, b, g),
        in_specs=[pl.BlockSpec((1, 1, 1, nc, width), lambda s, bb, gg: (s, bb, gg, 0, 0)),
                  pl.BlockSpec((1, 8, width * 2), lambda s, bb, gg: (s, 0, 0)),
                  pl.BlockSpec((1, width * 2, CMP_HIDDEN), lambda s, bb, gg: (s, 0, 0)),
                  pl.BlockSpec((1, CMP_HIDDEN, NSA_HEAD_DIM), lambda s, bb, gg: (s, 0, 0))],
        out_specs=pl.BlockSpec((1, 1, 1, nc, NSA_HEAD_DIM), lambda s, bb, gg: (s, bb, gg, 0, 0)),
        out_shape=jax.ShapeDtypeStruct((2, b, g, nc, NSA_HEAD_DIM), BF16),
        compiler_params=pltpu.CompilerParams(dimension_semantics=("arbitrary",) * 3),
        name="nsa_compress",
    )(a2, pos8, w1, w2)


def _t_cols(x):
    n = x.shape[1] // LANES
    return jnp.concatenate([x[:, c * LANES:(c + 1) * LANES].T for c in range(n)], axis=0)


def _flash_update(par, st, v_aug, m_sc, acc_sc):
    m_old = m_sc[par]
    m_new = jnp.maximum(m_old, jnp.max(st, axis=0, keepdims=True))
    p = jnp.exp2(st - m_new)
    acc_sc[par] = acc_sc[par] * jnp.exp2(m_old - m_new) + _dot(v_aug, p.astype(BF16))
    m_sc[par] = m_new


def _normalised(acc, par):
    half = LANES // 2
    if par == 0:
        return acc[:half] * (1.0 / acc[half:half + 1])
    return acc[half:] * (1.0 / acc[0:1])


def _nsa_attn_kernel(q_ref, gate_ref, ovl_ref, kca_ref, kcb_ref, vc_ref,
                     ksa_ref, ksb_ref, vse_ref, vso_ref, kwa_ref, kwb_ref, vwe_ref, vwo_ref,
                     o_ref, qm_sc, qaug_sc, x_sc, cnt_sc, tot_sc, m_sc, acc_sc, sta_sc, stb_sc,
                     *, qt, kt, n_blocks, n_sel, n_back):
    qi = pl.program_id(2)
    t0 = qi * qt
    half = LANES // 2
    wide = 2 * qt
    lane = lax.broadcasted_iota(jnp.int32, (qt, LANES), 1)
    low = lane < NSA_HEAD_DIM
    zero = jnp.zeros((qt, LANES), BF16)
    qps = [q_ref[0, :, pair * LANES:(pair + 1) * LANES] for pair in range(2)]
    for pair in range(2):
        rows = slice(pair * qt, (pair + 1) * qt)
        qm_sc[0, rows] = jnp.where(low, qps[pair], zero)
        qm_sc[1, rows] = jnp.where(low, zero, qps[pair])

    gates_t = _t_rows(gate_ref[0])

    def gate_row(branch, par):
        return jnp.concatenate([gates_t[branch * 4 + 2 * pair + par:branch * 4 + 2 * pair + par + 1, :]
                                for pair in range(2)], axis=1)

    col = lax.broadcasted_iota(jnp.int32, (1, wide), 1)
    tcol = t0 + jnp.where(col >= qt, col - qt, col)

    nc = kca_ref.shape[2]
    kidx = lax.broadcasted_iota(jnp.int32, (nc, wide), 0)
    valid = kidx * CMP_STRIDE + (CMP_BLOCK - 1) <= tcol
    vct = vc_ref[0, 0]
    has_block = tcol >= CMP_BLOCK - 1
    psum = None
    for par in range(2):
        st = _nt_dot((kcb_ref if par else kca_ref)[0, 0], qm_sc[par])
        st = jnp.where(valid, st, NEG)
        m = jnp.max(st, axis=0, keepdims=True)
        e = jnp.exp2(st - m)
        l = jnp.sum(e, axis=0, keepdims=True)
        p = e * jnp.where(has_block, 1.0 / l, 0.0)
        ph = p[:, :qt] + p[:, qt:]
        psum = ph if psum is None else psum + ph
        o_t = _dot(vct, p.astype(BF16))
        tot_sc[par] = o_t[:half] * gate_row(0, par)

    ovl = ovl_ref[...]
    hi = psum.astype(BF16)
    r1 = psum - hi.astype(F32)
    mid = r1.astype(BF16)
    lo = (r1 - mid.astype(F32)).astype(BF16)
    imp = (_dot(ovl, hi) + _dot(ovl, mid) + _dot(ovl, lo))[:half, :]
    nio = lax.broadcasted_iota(jnp.int32, (half, qt), 0)
    cur = lax.shift_right_logical(t0 + lax.broadcasted_iota(jnp.int32, (half, qt), 1), 6)
    forced = (nio == 0) | (nio == cur) | (nio == cur - 1)
    x_sc[...] = jnp.where(nio > cur, NEG, jnp.where(forced, imp + FORCE, imp))
    cnt_sc[...] = jnp.zeros_like(cnt_sc)

    cur_max = lax.shift_right_logical(t0 + qt - 1, 6)
    sub = lax.broadcasted_iota(jnp.int32, (SUBLANES, qt), 0)
    for rm in range(-(-n_blocks // SUBLANES)):
        @pl.when(rm * SUBLANES <= cur_max)
        def _(rm=rm):
            rows = x_sc[rm * SUBLANES:(rm + 1) * SUBLANES, :]
            for r in range(half // SUBLANES):
                grp = slice(r * SUBLANES, (r + 1) * SUBLANES)
                xr = x_sc[grp, :]
                cnt = cnt_sc[grp, :]
                for mm in range(SUBLANES):
                    row = rows[mm:mm + 1, :]
                    if r < rm:
                        beats = row > xr
                    elif r > rm:
                        beats = row >= xr
                    else:
                        beats = (row > xr) | ((row == xr) & (sub > mm))
                    cnt = cnt + jnp.where(beats, 1, 0)
                cnt_sc[grp, :] = cnt
    bias_t = jnp.where(cnt_sc[...] < n_sel, 0.0, NEG).astype(F32)
    bias = _t_cols(jnp.concatenate([bias_t, bias_t], axis=0)).astype(BF16)
    for pair in range(2):
        rows = slice(pair * qt, (pair + 1) * qt)
        qaug_sc[0, rows] = jnp.where(low, qps[pair], bias)
        qaug_sc[1, rows] = jnp.where(low, bias, qps[pair])

    ri = lax.broadcasted_iota(jnp.int32, (qt, wide), 0)
    ci = lax.broadcasted_iota(jnp.int32, (qt, wide), 1)
    ci = jnp.where(ci >= qt, ci - qt, ci)
    keeps, widx = [], []
    for back in range(n_back + 1):
        present = qi >= back
        widx.append(jnp.maximum(qi - back, 0))
        if back == 0:
            keeps.append(ri <= ci)
        elif back == n_back:
            keeps.append((ri > ci) & present)
        else:
            keeps.append(jnp.broadcast_to(present, (qt, wide)))
    for par in range(2):
        k_ref = kwb_ref if par else kwa_ref
        v_ref = vwo_ref if par else vwe_ref
        sts = []
        for back in range(n_back + 1):
            ks = pl.multiple_of(widx[back] * qt, qt)
            st = _nt_dot(k_ref[0, pl.ds(ks, qt), :], qm_sc[par])
            sts.append(jnp.where(keeps[back], st, NEG))
        m = functools.reduce(jnp.maximum, [jnp.max(st, axis=0, keepdims=True) for st in sts])
        acc = None
        for back in range(n_back + 1):
            part = _dot(v_ref[0, widx[back]], jnp.exp2(sts[back] - m).astype(BF16))
            acc = part if acc is None else acc + part
        tot_sc[par] = tot_sc[par] + _normalised(acc, par) * gate_row(2, par)

    m_sc[...] = jnp.full(m_sc.shape, NEG, F32)
    acc_sc[...] = jnp.zeros_like(acc_sc)

    def scores(kidx, buf):
        ks = pl.multiple_of(kidx * kt, kt)
        buf[0] = _nt_dot(ksa_ref[0, pl.ds(ks, kt), :], qaug_sc[0])
        buf[1] = _nt_dot(ksb_ref[0, pl.ds(ks, kt), :], qaug_sc[1])

    def consume(kidx, buf, masked):
        v_aug = (vse_ref[0, kidx], vso_ref[0, kidx])
        if masked:
            keep = kidx * kt + lax.broadcasted_iota(jnp.int32, (kt, wide), 0) <= tcol
        for par in range(2):
            st = buf[par]
            if masked:
                st = jnp.where(keep, st, NEG)
            _flash_update(par, st, v_aug[par], m_sc, acc_sc)

    assert qt <= kt
    last = (t0 + qt - 1) // kt
    scores(0, sta_sc)

    def body(j, carry):
        scores(2 * j + 1, stb_sc)
        consume(2 * j, sta_sc, False)
        scores(2 * j + 2, sta_sc)
        consume(2 * j + 1, stb_sc, False)
        return carry

    lax.fori_loop(0, last // 2, body, 0)

    @pl.when(last % 2 == 0)
    def _():
        consume(last, sta_sc, True)

    @pl.when(last % 2 == 1)
    def _():
        scores(last, stb_sc)
        consume(last - 1, sta_sc, False)
        consume(last, stb_sc, True)

    total = [tot_sc[par] + _normalised(acc_sc[par], par) * gate_row(1, par) for par in range(2)]
    for pair in range(2):
        cols = slice(pair * qt, (pair + 1) * qt)
        pair_t = jnp.concatenate([total[0][:, cols], total[1][:, cols]], axis=0)
        o_ref[0, :, pair * LANES:(pair + 1) * LANES] = _t_cols(pair_t).astype(BF16)


def _nsa_attn(qk3, gates3, ovl, kca, kcb, vc2, ksa, ksb, vse, vso, kwa, kwb, vwe, vwo, qt, kt):
    b, t, _ = qk3.shape
    g = NSA_KV_HEADS
    nc = kca.shape[2]
    n_blocks = t // SLC_BLOCK
    assert n_blocks <= LANES // 2 and SLC_BLOCK == 64 and WINDOW % qt == 0
    kern = functools.partial(_nsa_attn_kernel, qt=qt, kt=kt, n_blocks=n_blocks,
                             n_sel=min(SLC_TOPK, n_blocks), n_back=WINDOW // qt)
    per_group = lambda *tail: (lambda bb, gg, tt: (bb, gg) + tail)
    kc_spec = pl.BlockSpec((1, 1, nc, LANES), per_group(0, 0))
    k_spec = pl.BlockSpec((1, t, LANES), lambda bb, gg, tt: (gg, bb, 0))
    vs_spec = pl.BlockSpec((1, t // kt, LANES, kt), lambda bb, gg, tt: (gg, bb, 0, 0))
    vw_spec = pl.BlockSpec((1, t // qt, LANES, qt), lambda bb, gg, tt: (gg, bb, 0, 0))
    half = LANES // 2
    return pl.pallas_call(
        kern,
        grid=(b, g, t // qt),
        in_specs=[pl.BlockSpec((1, qt, MXU_DIM), lambda bb, gg, tt: (bb, tt, gg)),
                  pl.BlockSpec((1, qt, LANES), lambda bb, gg, tt: (bb, tt, gg)),
                  pl.BlockSpec((LANES, nc), lambda bb, gg, tt: (0, 0)),
                  kc_spec, kc_spec,
                  pl.BlockSpec((1, 1, LANES, nc), per_group(0, 0)),
                  k_spec, k_spec, vs_spec, vs_spec,
                  k_spec, k_spec, vw_spec, vw_spec],
        out_specs=pl.BlockSpec((1, qt, MXU_DIM), lambda bb, gg, tt: (bb, tt, gg)),
        out_shape=jax.ShapeDtypeStruct((b, t, NSA_QD), BF16),
        scratch_shapes=[pltpu.VMEM((2, 2 * qt, LANES), BF16),
                        pltpu.VMEM((2, 2 * qt, LANES), BF16),
                        pltpu.VMEM((half, qt), F32),
                        pltpu.VMEM((half, qt), jnp.int32),
                        pltpu.VMEM((2, half, 2 * qt), F32),
                        pltpu.VMEM((2, 1, 2 * qt), F32),
                        pltpu.VMEM((2, LANES, 2 * qt), F32),
                        pltpu.VMEM((2, kt, 2 * qt), F32),
                        pltpu.VMEM((2, kt, 2 * qt), F32)],
        compiler_params=pltpu.CompilerParams(dimension_semantics=("arbitrary",) * 3,
                                             vmem_limit_bytes=VMEM_LIMIT),
        name="nsa_attn",
    )(qk3, gates3, ovl, kca, kcb, vc2, ksa, ksb, vse, vso, kwa, kwb, vwe, vwo)


def _ret_rope_consts():
    half = RET_QK_DIM // 2
    inv_freq = RET_ROPE_THETA ** (-2.0 * jnp.arange(half, dtype=F32) / RET_QK_DIM)
    rows = jnp.zeros((8, LANES), F32)
    return rows.at[0].set(inv_freq).at[1].set(1.0)


def _nsa_rope_consts():
    half = NSA_ROPE_DIM // 2
    inv_freq = NSA_ROPE_THETA ** (-2.0 * jnp.arange(half, dtype=F32) / NSA_ROPE_DIM)
    r = np.arange(LANES) % NSA_HEAD_DIM
    rot = r < NSA_ROPE_DIM
    freq = jnp.where(rot, inv_freq[r % half], 0.0)
    mask_a = ((r >= half) & rot).astype(np.float32)
    mask_b = -(r < half).astype(np.float32)
    rows = jnp.zeros((8, LANES), F32)
    return rows.at[0].set(freq).at[1].set(mask_a).at[2].set(mask_b)


def _nsa_weight_columns():
    kv0 = NSA_QD
    cols = list(range(NSA_QD))
    for kvsel in range(2):
        for branch in range(3):
            base = kv0 + (branch * 2 + kvsel) * NSA_KVD
            cols += list(range(base, base + NSA_KVD))
    gate0 = NSA_QD + 6 * NSA_KVD
    hg = NSA_HEADS // NSA_KV_HEADS
    for g in range(NSA_KV_HEADS):
        lanes = [-1] * LANES
        for branch in range(3):
            for hh in range(hg):
                lanes[branch * 4 + hh] = gate0 + (g * hg + hh) * 3 + branch
        cols += lanes
    return np.asarray(cols, np.int32)


def _group_major(a, b, t):
    return a.reshape(b, t, NSA_KV_HEADS, NSA_HEAD_DIM).transpose(0, 2, 1, 3)


def _pad_pair(k):
    z = jnp.zeros_like(k)
    return jnp.concatenate([k, z], axis=-1), jnp.concatenate([z, k], axis=-1)


def _retention_layer(x2d, b, t, gain, w_in, w_out, cos, sin, tm):
    qk, v, gate = _ret_proj(x2d, gain.reshape(1, D_MODEL), w_in.astype(BF16), cos, sin, tm)
    wide = lambda a: a.reshape(b, t, 2 * D_MODEL)
    o = _retention(wide(qk), wide(v), wide(gate), min(t, 1024))
    return _matmul_res(o.reshape(b * t, RET_HEADS * RET_V_DIM), w_out.astype(BF16), x2d, tm)


def _nsa_layer(x2d, b, t, gain, w_in, cmp_pos, cmp_w1, cmp_w2, w_out, tabs, tm):
    g = NSA_KV_HEADS
    cols = _nsa_weight_columns()
    w_ext = jnp.concatenate([w_in, jnp.zeros((D_MODEL, 1), w_in.dtype)], axis=1)
    w_re = jnp.take(w_ext, jnp.asarray(np.where(cols < 0, w_in.shape[1], cols)), axis=1).astype(BF16)
    qt = min(t, 256)
    kt = min(t, 512)
    q, k_cmp_tok, v_cmp_tok, gates, *slc_win = _nsa_proj(
        x2d, gain.reshape(1, D_MODEL), w_re[:, :NSA_QK], w_re[:, NSA_QK:NSA_QK + NSA_V],
        w_re[:, NSA_QK + NSA_V:], *tabs, tm, t, kt, qt)
    ksa, ksb, kwa, kwb, vse, vso, vwe, vwo = slc_win

    nc = t // CMP_STRIDE
    a2 = jnp.stack([_group_major(k_cmp_tok, b, t), _group_major(v_cmp_tok, b, t)])
    a2 = a2.reshape(2, b, g, nc, CMP_STRIDE * NSA_HEAD_DIM)
    pos8 = jnp.broadcast_to(cmp_pos.reshape(2, 1, CMP_BLOCK * NSA_HEAD_DIM),
                            (2, 8, CMP_BLOCK * NSA_HEAD_DIM)).astype(BF16)
    cmp = _compress(a2, pos8, cmp_w1.astype(BF16), cmp_w2.astype(BF16))
    kca, kcb = _pad_pair(cmp[0])
    vct = cmp[1].transpose(0, 1, 3, 2)
    vc2 = jnp.concatenate([vct, vct], axis=2)

    n_blocks = t // SLC_BLOCK
    kk = np.arange(nc)[None, :]
    nn = np.arange(LANES)[:, None]
    ovl = ((kk * CMP_STRIDE < nn * SLC_BLOCK + SLC_BLOCK) & (kk * CMP_STRIDE + CMP_BLOCK > nn * SLC_BLOCK)
           & (nn < n_blocks) & (kk < nc - 1))
    ovl = jnp.asarray(ovl, BF16)

    attn = _nsa_attn(q.reshape(b, t, NSA_QD), gates.reshape(b, t, NSA_GATE_PAD), ovl, kca, kcb, vc2,
                     ksa, ksb, vse, vso, kwa, kwb, vwe, vwo, qt, kt)
    return _matmul_res(attn.reshape(b * t, NSA_QD), w_out.astype(BF16), x2d, tm)


def _ffn_layer(x2d, gain, w_gu, w_down, tm, final_gain=None):
    fg = None if final_gain is None else final_gain.reshape(1, D_MODEL)
    return _ffn(x2d, gain.reshape(1, D_MODEL), w_gu.astype(BF16), w_down.astype(BF16), tm, final_gain=fg)


def kernel(x, positions, norm_mix, norm_ffn, norm_final, ret_w_in, ret_w_out, nsa_w_in, nsa_cmp_pos,
           nsa_cmp_w1, nsa_cmp_w2, nsa_w_out, ffn_w_gu, ffn_w_down):
    b, t, _ = x.shape
    m = b * t
    tm = min(m, ROW_TILE)
    x2d = x.reshape(m, D_MODEL)
    pos_col = positions.reshape(m, 1).astype(jnp.int32)
    ret_cos, ret_sin, _ = _rope_tables(pos_col, _ret_rope_consts(), tm)
    nsa_tabs = _rope_tables(pos_col, _nsa_rope_consts(), tm)
    for i in range(DEPTH):
        j = i // 2
        if i % 2 == 0:
            x2d = _retention_layer(x2d, b, t, norm_mix[i], ret_w_in[j], ret_w_out[j], ret_cos, ret_sin, tm)
        else:
            x2d = _nsa_layer(x2d, b, t, norm_mix[i], nsa_w_in[j], nsa_cmp_pos[j], nsa_cmp_w1[j],
                             nsa_cmp_w2[j], nsa_w_out[j], nsa_tabs, tm)
        x2d = _ffn_layer(x2d, norm_ffn[i], ffn_w_gu[i], ffn_w_down[i], tm,
                         final_gain=norm_final if i == DEPTH - 1 else None)
    return x2d.reshape(b, t, D_MODEL)
```

```python
import functools

import numpy as np
import jax
import jax.numpy as jnp
from jax import lax
from jax.experimental import pallas as pl
from jax.experimental.pallas import tpu as pltpu

F32 = jnp.float32
BF16 = jnp.bfloat16

D_MODEL = 1024
DEPTH = 4
NORM_EPS = 1e-6
NEG = -1e30

RET_HEADS = 4
RET_QK_DIM = 256
RET_V_DIM = 512
RET_ROPE_THETA = 10000.0
RET_CHUNK = 256

NSA_HEADS = 16
NSA_KV_HEADS = 4
NSA_HEAD_DIM = 64
NSA_ROPE_THETA = 500000.0
NSA_ROPE_DIM = 16
CMP_BLOCK = 32
CMP_STRIDE = 16
CMP_HIDDEN = 256
SLC_BLOCK = 64
SLC_TOPK = 16
WINDOW = 512
FORCE = 1e4
NSA_QD = 1024
NSA_KVD = 256
FFN_HIDDEN = 2816

LANES = 128
SUBLANES = 8
MXU_DIM = 256
ROW_TILE = 512
VMEM_LIMIT = 56 * 1024 * 1024


def _nt_dot(a, b):
    return lax.dot_general(a, b, (((1,), (1,)), ((), ())), preferred_element_type=F32)


def _dot(a, b):
    return jnp.dot(a, b, preferred_element_type=F32)


def _rms_to_bf16(x, gain):
    ms = jnp.mean(x * x, axis=-1, keepdims=True)
    return (x * lax.rsqrt(ms + NORM_EPS) * gain).astype(BF16)


def _resident(shape, index_map):
    return pl.BlockSpec(shape, index_map, pipeline_mode=pl.Buffered(1))


def _rope_tables_kernel(pos_ref, c_ref, cos_ref, sa_ref, sb_ref):
    ang = pos_ref[...].astype(F32) * c_ref[0:1, :]
    s = jnp.sin(ang)
    cos_ref[...] = jnp.cos(ang)
    sa_ref[...] = s * c_ref[1:2, :]
    sb_ref[...] = s * c_ref[2:3, :]


def _rope_tables(pos_col, consts, tm):
    m = pos_col.shape[0]
    out = jax.ShapeDtypeStruct((m, LANES), F32)
    return pl.pallas_call(
        _rope_tables_kernel,
        grid=(m // tm,),
        in_specs=[pl.BlockSpec((tm, 1), lambda i: (i, 0)),
                  pl.BlockSpec((8, LANES), lambda i: (0, 0))],
        out_specs=[pl.BlockSpec((tm, LANES), lambda i: (i, 0))] * 3,
        out_shape=[out, out, out],
        name="rope_tables",
    )(pos_col, consts)


def _ret_proj_kernel(x_ref, gain_ref, wqk_ref, wv_ref, wg_ref, cos_ref, sin_ref, qk_ref, v_ref, g_ref):
    hn = _rms_to_bf16(x_ref[...], gain_ref[...])
    qk = _dot(hn, wqk_ref[...])
    c = cos_ref[...]
    s = sin_ref[...]
    for head in range(2 * RET_HEADS):
        scale = 1.0 if head < RET_HEADS else RET_QK_DIM ** -0.5
        lo = head * RET_QK_DIM
        x1 = qk[:, lo:lo + LANES]
        x2 = qk[:, lo + LANES:lo + 2 * LANES]
        qk_ref[:, lo:lo + LANES] = ((x1 * c - x2 * s) * scale).astype(BF16)
        qk_ref[:, lo + LANES:lo + 2 * LANES] = ((x2 * c + x1 * s) * scale).astype(BF16)
    v_ref[...] = _dot(hn, wv_ref[...]).astype(BF16)
    gate = _dot(hn, wg_ref[...])
    g_ref[...] = (gate * jax.nn.sigmoid(gate)).astype(BF16)


def _ret_proj(x2d, gain, w_bf16, cos, sin, tm):
    m = x2d.shape[0]
    width = 2 * D_MODEL
    out = jax.ShapeDtypeStruct((m, width), BF16)
    return pl.pallas_call(
        _ret_proj_kernel,
        grid=(m // tm,),
        in_specs=[pl.BlockSpec((tm, D_MODEL), lambda i: (i, 0)),
                  _resident((1, D_MODEL), lambda i: (0, 0)),
                  _resident((D_MODEL, width), lambda i: (0, 0)),
                  _resident((D_MODEL, width), lambda i: (0, 1)),
                  _resident((D_MODEL, width), lambda i: (0, 2)),
                  pl.BlockSpec((tm, LANES), lambda i: (i, 0)),
                  pl.BlockSpec((tm, LANES), lambda i: (i, 0))],
        out_specs=[pl.BlockSpec((tm, width), lambda i: (i, 0))] * 3,
        out_shape=[out, out, out],
        compiler_params=pltpu.CompilerParams(dimension_semantics=("arbitrary",),
                                             vmem_limit_bytes=VMEM_LIMIT),
        name="ret_proj",
    )(x2d, gain, w_bf16, w_bf16, w_bf16, cos, sin)


def _retention_kernel(q_ref, k_ref, v_ref, g_ref, o_ref, state, intra, qd, kd, *, chunk, n_chunks):
    c_len = chunk
    h = pl.program_id(1)
    t = pl.program_id(2)

    @pl.when(t == 0)
    def _():
        hv = jnp.full((c_len, RET_QK_DIM), h, jnp.int32)
        den = jnp.left_shift(jnp.full((c_len, RET_QK_DIM), 32, jnp.int32), hv).astype(F32)
        lg = jnp.log(1.0 - 1.0 / den)
        ri = lax.broadcasted_iota(jnp.int32, (c_len, RET_QK_DIM), 0)
        ci = lax.broadcasted_iota(jnp.int32, (c_len, RET_QK_DIM), 1)
        diff = (ri - ci).astype(F32)
        intra[...] = jnp.where(diff >= 0, jnp.exp(lg * jnp.maximum(diff, 0.0)), 0.0)
        rf = ri.astype(F32)
        qd[...] = jnp.exp(lg * (rf + 1.0))
        kd[...] = jnp.exp(lg * (c_len - 1.0 - rf))
        state[...] = jnp.zeros_like(state)

    chunk_decay = qd[c_len - 1:c_len, 0:1]
    for c in range(n_chunks):
        sl = pl.ds(c * c_len, c_len)
        qc = q_ref[0, sl, :]
        kc = k_ref[0, sl, :]
        vc = v_ref[0, sl, :]
        s = _nt_dot(qc, kc) * intra[...]
        st = state[...]
        qdec = (qc.astype(F32) * qd[...]).astype(BF16)
        o = _dot(s.astype(BF16), vc) + _dot(qdec, st.astype(BF16))
        kdec = (kc.astype(F32) * kd[...]).astype(BF16)
        state[...] = st * chunk_decay + lax.dot_general(
            kdec, vc, (((0,), (0,)), ((), ())), preferred_element_type=F32)
        ms = jnp.mean(o * o, axis=-1, keepdims=True)
        y = o * lax.rsqrt(ms + NORM_EPS)
        o_ref[0, sl, :] = (y * g_ref[0, sl, :].astype(F32)).astype(BF16)


def _retention(qk3, v3, g3, tq):
    b, t, _ = qk3.shape
    chunk = RET_CHUNK
    assert RET_CHUNK == RET_QK_DIM and tq % chunk == 0 and t % tq == 0
    kern = functools.partial(_retention_kernel, chunk=chunk, n_chunks=tq // chunk)
    nq = D_MODEL // RET_QK_DIM
    return pl.pallas_call(
        kern,
        grid=(b, RET_HEADS, t // tq),
        in_specs=[pl.BlockSpec((1, tq, RET_QK_DIM), lambda bb, h, tt: (bb, tt, h)),
                  pl.BlockSpec((1, tq, RET_QK_DIM), lambda bb, h, tt: (bb, tt, nq + h)),
                  pl.BlockSpec((1, tq, RET_V_DIM), lambda bb, h, tt: (bb, tt, h)),
                  pl.BlockSpec((1, tq, RET_V_DIM), lambda bb, h, tt: (bb, tt, h))],
        out_specs=pl.BlockSpec((1, tq, RET_V_DIM), lambda bb, h, tt: (bb, tt, h)),
        out_shape=jax.ShapeDtypeStruct((b, t, RET_HEADS * RET_V_DIM), BF16),
        scratch_shapes=[pltpu.VMEM((RET_QK_DIM, RET_V_DIM), F32),
                        pltpu.VMEM((chunk, RET_QK_DIM), F32),
                        pltpu.VMEM((chunk, RET_QK_DIM), F32),
                        pltpu.VMEM((chunk, RET_QK_DIM), F32)],
        compiler_params=pltpu.CompilerParams(dimension_semantics=("arbitrary",) * 3),
        name="retention",
    )(qk3, qk3, v3, g3)


def _matmul_res_kernel(a_ref, w_ref, res_ref, o_ref):
    o_ref[...] = res_ref[...] + _dot(a_ref[...], w_ref[...])


def _matmul_res(a, w_bf16, res, tm):
    m, k = a.shape
    n = w_bf16.shape[1]
    return pl.pallas_call(
        _matmul_res_kernel,
        grid=(m // tm,),
        in_specs=[pl.BlockSpec((tm, k), lambda i: (i, 0)),
                  _resident((k, n), lambda i: (0, 0)),
                  pl.BlockSpec((tm, n), lambda i: (i, 0))],
        out_specs=pl.BlockSpec((tm, n), lambda i: (i, 0)),
        out_shape=jax.ShapeDtypeStruct((m, n), F32),
        compiler_params=pltpu.CompilerParams(dimension_semantics=("arbitrary",)),
        name="matmul_res",
    )(a, w_bf16, res)


def _ffn_kernel(x_ref, gain_ref, wg_ref, wu_ref, wd_ref, *rest, final_norm):
    if final_norm:
        fgain_ref, o_ref = rest
    else:
        (o_ref,) = rest
    x = x_ref[...]
    hn = _rms_to_bf16(x, gain_ref[...])
    a = _dot(hn, wg_ref[...])
    b = _dot(hn, wu_ref[...])
    hid = (a * jax.nn.sigmoid(a) * b).astype(BF16)
    y = x + _dot(hid, wd_ref[...])
    if final_norm:
        ms = jnp.mean(y * y, axis=-1, keepdims=True)
        y = y * lax.rsqrt(ms + NORM_EPS) * fgain_ref[...]
    o_ref[...] = y


def _ffn(x2d, gain, w_gu_bf16, w_down_bf16, tm, final_gain=None):
    m = x2d.shape[0]
    in_specs = [pl.BlockSpec((tm, D_MODEL), lambda i: (i, 0)),
                _resident((1, D_MODEL), lambda i: (0, 0)),
                _resident((D_MODEL, FFN_HIDDEN), lambda i: (0, 0)),
                _resident((D_MODEL, FFN_HIDDEN), lambda i: (0, 1)),
                _resident((FFN_HIDDEN, D_MODEL), lambda i: (0, 0))]
    args = [x2d, gain, w_gu_bf16, w_gu_bf16, w_down_bf16]
    if final_gain is not None:
        in_specs.append(_resident((1, D_MODEL), lambda i: (0, 0)))
        args.append(final_gain)
    return pl.pallas_call(
        functools.partial(_ffn_kernel, final_norm=final_gain is not None),
        grid=(m // tm,),
        in_specs=in_specs,
        out_specs=pl.BlockSpec((tm, D_MODEL), lambda i: (i, 0)),
        out_shape=jax.ShapeDtypeStruct((m, D_MODEL), F32),
        compiler_params=pltpu.CompilerParams(dimension_semantics=("arbitrary",),
                                             vmem_limit_bytes=VMEM_LIMIT),
        name="ffn",
    )(*args)


NSA_QK = NSA_QD + 3 * NSA_KVD
NSA_V = 3 * NSA_KVD
NSA_GATE_PAD = NSA_KV_HEADS * LANES
Q_SCALE_LOG2 = NSA_HEAD_DIM ** -0.5 * float(np.log2(np.e))


def _partial_rotary(a, c, sa, sb):
    return a * c + pltpu.roll(a, 8, 1) * sa + pltpu.roll(a, LANES - 8, 1) * sb


def _t_rows(x):
    n = x.shape[0] // LANES
    return jnp.concatenate([x[c * LANES:(c + 1) * LANES, :].T for c in range(n)], axis=1)


def _nsa_proj_kernel(x_ref, gain_ref, wqk_ref, wv_ref, wg_ref, c_ref, sa_ref, sb_ref,
                     q_ref, kc_ref, vc_ref, gate_ref, ksa_ref, ksb_ref, kwa_ref, kwb_ref,
                     vse_ref, vso_ref, vwe_ref, vwo_ref, *, t, kt, qt):
    tm = x_ref.shape[0]
    half = LANES // 2
    hn = _rms_to_bf16(x_ref[...], gain_ref[...])
    qk = _dot(hn, wqk_ref[...])
    c = c_ref[...]
    sa = sa_ref[...]
    sb = sb_ref[...]

    def rotated(lo, scale=1.0):
        return _partial_rotary(qk[:, lo:lo + LANES], c, sa, sb) * scale

    for chunk in range(NSA_QD // LANES):
        q_ref[:, chunk * LANES:(chunk + 1) * LANES] = rotated(chunk * LANES, Q_SCALE_LOG2).astype(BF16)
    for chunk in range(NSA_KVD // LANES):
        kc_ref[:, chunk * LANES:(chunk + 1) * LANES] = rotated(NSA_QD + chunk * LANES).astype(BF16)

    lane = lax.broadcasted_iota(jnp.int32, (tm, LANES), 1)
    low = lane < half
    pos = lax.rem(pl.program_id(0) * tm + lax.broadcasted_iota(jnp.int32, (tm, LANES), 0), t)
    onehot = jnp.where((lane & (half - 1)) == lax.shift_right_logical(pos, 6), 1.0, 0.0)
    zero = jnp.zeros((tm, LANES), F32)
    for branch, (a_ref, b_ref, fill) in enumerate(((ksa_ref, ksb_ref, onehot), (kwa_ref, kwb_ref, zero))):
        base = NSA_QD + (1 + branch) * NSA_KVD
        for chunk in range(NSA_KVD // LANES):
            kk = rotated(base + chunk * LANES)
            swapped = pltpu.roll(kk, half, 1)
            a_ref[2 * chunk] = jnp.where(low, kk, fill).astype(BF16)
            b_ref[2 * chunk] = jnp.where(low, fill, swapped).astype(BF16)
            a_ref[2 * chunk + 1] = jnp.where(low, swapped, fill).astype(BF16)
            b_ref[2 * chunk + 1] = jnp.where(low, fill, kk).astype(BF16)

    v = _dot(hn, wv_ref[...])
    vc_ref[...] = v[:, :NSA_KVD].astype(BF16)
    ones = jnp.ones((half, tm), F32)
    for branch, (e_ref, o_ref, tile) in enumerate(((vse_ref, vso_ref, kt), (vwe_ref, vwo_ref, qt))):
        base = (1 + branch) * NSA_KVD
        for chunk in range(NSA_KVD // LANES):
            v_t = _t_rows(v[:, base + chunk * LANES:base + (chunk + 1) * LANES])
            for par in range(2):
                vg = v_t[par * half:(par + 1) * half]
                even = jnp.concatenate([vg, ones], axis=0).astype(BF16)
                odd = jnp.concatenate([ones, vg], axis=0).astype(BF16)
                for j in range(tm // tile):
                    e_ref[2 * chunk + par, j] = even[:, j * tile:(j + 1) * tile]
                    o_ref[2 * chunk + par, j] = odd[:, j * tile:(j + 1) * tile]
    gate_ref[...] = jax.nn.sigmoid(_dot(hn, wg_ref[...]))


def _nsa_proj(x2d, gain, wqk, wv, wg, c, sa, sb, tm, t, kt, qt):
    m = x2d.shape[0]
    g = NSA_KV_HEADS
    assert SLC_BLOCK == 64 and t % tm == 0 and tm % kt == 0 and tm % qt == 0
    row = lambda i: (i, 0)
    fixed = lambda i: (0, 0)
    key_spec = pl.BlockSpec((g, tm, LANES), lambda i: (0, i, 0))
    key_shape = jax.ShapeDtypeStruct((g, m, LANES), BF16)

    def val(tile):
        return (pl.BlockSpec((g, tm // tile, LANES, tile), lambda i: (0, i, 0, 0)),
                jax.ShapeDtypeStruct((g, m // tile, LANES, tile), BF16))

    return pl.pallas_call(
        functools.partial(_nsa_proj_kernel, t=t, kt=kt, qt=qt),
        grid=(m // tm,),
        in_specs=[pl.BlockSpec((tm, D_MODEL), row),
                  _resident((1, D_MODEL), fixed),
                  _resident((D_MODEL, NSA_QK), fixed),
                  _resident((D_MODEL, NSA_V), fixed),
                  _resident((D_MODEL, NSA_GATE_PAD), fixed),
                  pl.BlockSpec((tm, LANES), row),
                  pl.BlockSpec((tm, LANES), row),
                  pl.BlockSpec((tm, LANES), row)],
        out_specs=[pl.BlockSpec((tm, NSA_QD), row),
                   pl.BlockSpec((tm, NSA_KVD), row),
                   pl.BlockSpec((tm, NSA_KVD), row),
                   pl.BlockSpec((tm, NSA_GATE_PAD), row),
                   key_spec, key_spec, key_spec, key_spec,
                   val(kt)[0], val(kt)[0], val(qt)[0], val(qt)[0]],
        out_shape=[jax.ShapeDtypeStruct((m, NSA_QD), BF16),
                   jax.ShapeDtypeStruct((m, NSA_KVD), BF16),
                   jax.ShapeDtypeStruct((m, NSA_KVD), BF16),
                   jax.ShapeDtypeStruct((m, NSA_GATE_PAD), F32),
                   key_shape, key_shape, key_shape, key_shape,
                   val(kt)[1], val(kt)[1], val(qt)[1], val(qt)[1]],
        compiler_params=pltpu.CompilerParams(dimension_semantics=("arbitrary",),
                                             vmem_limit_bytes=VMEM_LIMIT),
        name="nsa_proj",
    )(x2d, gain, wqk, wv, wg, c, sa, sb)


def _compress_kernel(a_ref, pos_ref, w1_ref, w2_ref, o_ref):
    a = a_ref[0, 0, 0]
    nc = a.shape[0]
    half = CMP_STRIDE * NSA_HEAD_DIM
    upper = _dot(a, w1_ref[0, :half, :])
    lower = _dot(a, w1_ref[0, half:, :])
    cpos = _dot(pos_ref[0], w1_ref[0])
    hid = upper + pltpu.roll(lower, nc - 1, 0) + cpos[0:1, :]
    hid = hid * jax.nn.sigmoid(hid)
    o_ref[0, 0, 0] = _dot(hid.astype(BF16), w2_ref[0]).astype(BF16)


def _compress(a2, pos8, w1, w2):
    _, b, g, nc, width = a2.shape
    return pl.pallas_call(
        _compress_kernel,
        grid=(2, b, g),
        in_specs=[pl.BlockSpec((1, 1, 1, nc, width), lambda s, bb, gg: (s, bb, gg, 0, 0)),
                  pl.BlockSpec((1, 8, width * 2), lambda s, bb, gg: (s, 0, 0)),
                  pl.BlockSpec((1, width * 2, CMP_HIDDEN), lambda s, bb, gg: (s, 0, 0)),
                  pl.BlockSpec((1, CMP_HIDDEN, NSA_HEAD_DIM), lambda s, bb, gg: (s, 0, 0))],
        out_specs=pl.BlockSpec((1, 1, 1, nc, NSA_HEAD_DIM), lambda s, bb, gg: (s, bb, gg, 0, 0)),
        out_shape=jax.ShapeDtypeStruct((2, b, g, nc, NSA_HEAD_DIM), BF16),
        compiler_params=pltpu.CompilerParams(dimension_semantics=("arbitrary",) * 3),
        name="nsa_compress",
    )(a2, pos8, w1, w2)


def _t_cols(x):
    n = x.shape[1] // LANES
    return jnp.concatenate([x[:, c * LANES:(c + 1) * LANES].T for c in range(n)], axis=0)


def _flash_update(par, st, v_aug, m_sc, acc_sc):
    m_old = m_sc[par]
    m_new = jnp.maximum(m_old, jnp.max(st, axis=0, keepdims=True))
    p = jnp.exp2(st - m_new)
    acc_sc[par] = acc_sc[par] * jnp.exp2(m_old - m_new) + _dot(v_aug, p.astype(BF16))
    m_sc[par] = m_new


def _normalised(acc, par):
    half = LANES // 2
    if par == 0:
        return acc[:half] * (1.0 / acc[half:half + 1])
    return acc[half:] * (1.0 / acc[0:1])


def _nsa_attn_kernel(q_ref, gate_ref, ovl_ref, kca_ref, kcb_ref, vc_ref,
                     ksa_ref, ksb_ref, vse_ref, vso_ref, kwa_ref, kwb_ref, vwe_ref, vwo_ref,
                     o_ref, qm_sc, qaug_sc, x_sc, cnt_sc, tot_sc, m_sc, acc_sc, sta_sc, stb_sc,
                     *, qt, kt, n_blocks, n_sel, n_back):
    qi = pl.program_id(2)
    t0 = qi * qt
    half = LANES // 2
    wide = 2 * qt
    lane = lax.broadcasted_iota(jnp.int32, (qt, LANES), 1)
    low = lane < NSA_HEAD_DIM
    zero = jnp.zeros((qt, LANES), BF16)
    qps = [q_ref[0, :, pair * LANES:(pair + 1) * LANES] for pair in range(2)]
    for pair in range(2):
        rows = slice(pair * qt, (pair + 1) * qt)
        qm_sc[0, rows] = jnp.where(low, qps[pair], zero)
        qm_sc[1, rows] = jnp.where(low, zero, qps[pair])

    gates_t = _t_rows(gate_ref[0])

    def gate_row(branch, par):
        return jnp.concatenate([gates_t[branch * 4 + 2 * pair + par:branch * 4 + 2 * pair + par + 1, :]
                                for pair in range(2)], axis=1)

    col = lax.broadcasted_iota(jnp.int32, (1, wide), 1)
    tcol = t0 + jnp.where(col >= qt, col - qt, col)

    nc = kca_ref.shape[2]
    kidx = lax.broadcasted_iota(jnp.int32, (nc, wide), 0)
    valid = kidx * CMP_STRIDE + (CMP_BLOCK - 1) <= tcol
    vct = vc_ref[0, 0]
    has_block = tcol >= CMP_BLOCK - 1
    psum = None
    for par in range(2):
        st = _nt_dot((kcb_ref if par else kca_ref)[0, 0], qm_sc[par])
        st = jnp.where(valid, st, NEG)
        m = jnp.max(st, axis=0, keepdims=True)
        e = jnp.exp2(st - m)
        l = jnp.sum(e, axis=0, keepdims=True)
        p = e * jnp.where(has_block, 1.0 / l, 0.0)
        ph = p[:, :qt] + p[:, qt:]
        psum = ph if psum is None else psum + ph
        o_t = _dot(vct, p.astype(BF16))
        tot_sc[par] = o_t[:half] * gate_row(0, par)

    ovl = ovl_ref[...]
    hi = psum.astype(BF16)
    r1 = psum - hi.astype(F32)
    mid = r1.astype(BF16)
    lo = (r1 - mid.astype(F32)).astype(BF16)
    imp = (_dot(ovl, hi) + _dot(ovl, mid) + _dot(ovl, lo))[:half, :]
    nio = lax.broadcasted_iota(jnp.int32, (half, qt), 0)
    cur = lax.shift_right_logical(t0 + lax.broadcasted_iota(jnp.int32, (half, qt), 1), 6)
    forced = (nio == 0) | (nio == cur) | (nio == cur - 1)
    x_sc[...] = jnp.where(nio > cur, NEG, jnp.where(forced, imp + FORCE, imp))
    cnt_sc[...] = jnp.zeros_like(cnt_sc)

    cur_max = lax.shift_right_logical(t0 + qt - 1, 6)
    sub = lax.broadcasted_iota(jnp.int32, (SUBLANES, qt), 0)
    n_groups = half // SUBLANES
    xs = [x_sc[r * SUBLANES:(r + 1) * SUBLANES, :] for r in range(n_groups)]
    cnts = [jnp.zeros((SUBLANES, qt), jnp.int32) for _ in range(n_groups)]
    for rm in range(-(-n_blocks // SUBLANES)):
        for mm in range(SUBLANES):
            row = xs[rm][mm:mm + 1, :]
            for r in range(n_groups):
                if r < rm:
                    beats = row > xs[r]
                elif r > rm:
                    beats = row >= xs[r]
                else:
                    beats = (row > xs[r]) | ((row == xs[r]) & (sub > mm))
                cnts[r] = cnts[r] + jnp.where(beats, 1, 0)
    bias_t = jnp.where(jnp.concatenate(cnts, axis=0) < n_sel, 0.0, NEG).astype(F32)
    bias = _t_cols(jnp.concatenate([bias_t, bias_t], axis=0)).astype(BF16)
    for pair in range(2):
        rows = slice(pair * qt, (pair + 1) * qt)
        qaug_sc[0, rows] = jnp.where(low, qps[pair], bias)
        qaug_sc[1, rows] = jnp.where(low, bias, qps[pair])

    ri = lax.broadcasted_iota(jnp.int32, (qt, wide), 0)
    ci = lax.broadcasted_iota(jnp.int32, (qt, wide), 1)
    ci = jnp.where(ci >= qt, ci - qt, ci)
    keeps, widx = [], []
    for back in range(n_back + 1):
        present = qi >= back
        widx.append(jnp.maximum(qi - back, 0))
        if back == 0:
            keeps.append(ri <= ci)
        elif back == n_back:
            keeps.append((ri > ci) & present)
        else:
            keeps.append(jnp.broadcast_to(present, (qt, wide)))
    for par in range(2):
        k_ref = kwb_ref if par else kwa_ref
        v_ref = vwo_ref if par else vwe_ref
        sts = []
        for back in range(n_back + 1):
            ks = pl.multiple_of(widx[back] * qt, qt)
            st = _nt_dot(k_ref[0, pl.ds(ks, qt), :], qm_sc[par])
            sts.append(jnp.where(keeps[back], st, NEG))
        m = functools.reduce(jnp.maximum, [jnp.max(st, axis=0, keepdims=True) for st in sts])
        acc = None
        for back in range(n_back + 1):
            part = _dot(v_ref[0, widx[back]], jnp.exp2(sts[back] - m).astype(BF16))
            acc = part if acc is None else acc + part
        tot_sc[par] = tot_sc[par] + _normalised(acc, par) * gate_row(2, par)

    m_sc[...] = jnp.full(m_sc.shape, NEG, F32)
    acc_sc[...] = jnp.zeros_like(acc_sc)

    def scores(kidx, buf):
        ks = pl.multiple_of(kidx * kt, kt)
        buf[0] = _nt_dot(ksa_ref[0, pl.ds(ks, kt), :], qaug_sc[0])
        buf[1] = _nt_dot(ksb_ref[0, pl.ds(ks, kt), :], qaug_sc[1])

    def consume(kidx, buf, masked):
        v_aug = (vse_ref[0, kidx], vso_ref[0, kidx])
        if masked:
            keep = kidx * kt + lax.broadcasted_iota(jnp.int32, (kt, wide), 0) <= tcol
        for par in range(2):
            st = buf[par]
            if masked:
                st = jnp.where(keep, st, NEG)
            _flash_update(par, st, v_aug[par], m_sc, acc_sc)

    assert qt <= kt
    last = (t0 + qt - 1) // kt
    scores(0, sta_sc)

    def body(j, carry):
        scores(2 * j + 1, stb_sc)
        consume(2 * j, sta_sc, False)
        scores(2 * j + 2, sta_sc)
        consume(2 * j + 1, stb_sc, False)
        return carry

    lax.fori_loop(0, last // 2, body, 0)

    @pl.when(last % 2 == 0)
    def _():
        consume(last, sta_sc, True)

    @pl.when(last % 2 == 1)
    def _():
        scores(last, stb_sc)
        consume(last - 1, sta_sc, False)
        consume(last, stb_sc, True)

    total = [tot_sc[par] + _normalised(acc_sc[par], par) * gate_row(1, par) for par in range(2)]
    for pair in range(2):
        cols = slice(pair * qt, (pair + 1) * qt)
        pair_t = jnp.concatenate([total[0][:, cols], total[1][:, cols]], axis=0)
        o_ref[0, :, pair * LANES:(pair + 1) * LANES] = _t_cols(pair_t).astype(BF16)


def _nsa_attn(qk3, gates3, ovl, kca, kcb, vc2, ksa, ksb, vse, vso, kwa, kwb, vwe, vwo, qt, kt):
    b, t, _ = qk3.shape
    g = NSA_KV_HEADS
    nc = kca.shape[2]
    n_blocks = t // SLC_BLOCK
    assert n_blocks <= LANES // 2 and SLC_BLOCK == 64 and WINDOW % qt == 0
    kern = functools.partial(_nsa_attn_kernel, qt=qt, kt=kt, n_blocks=n_blocks,
                             n_sel=min(SLC_TOPK, n_blocks), n_back=WINDOW // qt)
    per_group = lambda *tail: (lambda bb, gg, tt: (bb, gg) + tail)
    kc_spec = pl.BlockSpec((1, 1, nc, LANES), per_group(0, 0))
    k_spec = pl.BlockSpec((1, t, LANES), lambda bb, gg, tt: (gg, bb, 0))
    vs_spec = pl.BlockSpec((1, t // kt, LANES, kt), lambda bb, gg, tt: (gg, bb, 0, 0))
    vw_spec = pl.BlockSpec((1, t // qt, LANES, qt), lambda bb, gg, tt: (gg, bb, 0, 0))
    half = LANES // 2
    return pl.pallas_call(
        kern,
        grid=(b, g, t // qt),
        in_specs=[pl.BlockSpec((1, qt, MXU_DIM), lambda bb, gg, tt: (bb, tt, gg)),
                  pl.BlockSpec((1, qt, LANES), lambda bb, gg, tt: (bb, tt, gg)),
                  pl.BlockSpec((LANES, nc), lambda bb, gg, tt: (0, 0)),
                  kc_spec, kc_spec,
                  pl.BlockSpec((1, 1, LANES, nc), per_group(0, 0)),
                  k_spec, k_spec, vs_spec, vs_spec,
                  k_spec, k_spec, vw_spec, vw_spec],
        out_specs=pl.BlockSpec((1, qt, MXU_DIM), lambda bb, gg, tt: (bb, tt, gg)),
        out_shape=jax.ShapeDtypeStruct((b, t, NSA_QD), BF16),
        scratch_shapes=[pltpu.VMEM((2, 2 * qt, LANES), BF16),
                        pltpu.VMEM((2, 2 * qt, LANES), BF16),
                        pltpu.VMEM((half, qt), F32),
                        pltpu.VMEM((half, qt), jnp.int32),
                        pltpu.VMEM((2, half, 2 * qt), F32),
                        pltpu.VMEM((2, 1, 2 * qt), F32),
                        pltpu.VMEM((2, LANES, 2 * qt), F32),
                        pltpu.VMEM((2, kt, 2 * qt), F32),
                        pltpu.VMEM((2, kt, 2 * qt), F32)],
        compiler_params=pltpu.CompilerParams(dimension_semantics=("arbitrary",) * 3,
                                             vmem_limit_bytes=VMEM_LIMIT),
        name="nsa_attn",
    )(qk3, gates3, ovl, kca, kcb, vc2, ksa, ksb, vse, vso, kwa, kwb, vwe, vwo)


def _ret_rope_consts():
    half = RET_QK_DIM // 2
    inv_freq = RET_ROPE_THETA ** (-2.0 * jnp.arange(half, dtype=F32) / RET_QK_DIM)
    rows = jnp.zeros((8, LANES), F32)
    return rows.at[0].set(inv_freq).at[1].set(1.0)


def _nsa_rope_consts():
    half = NSA_ROPE_DIM // 2
    inv_freq = NSA_ROPE_THETA ** (-2.0 * jnp.arange(half, dtype=F32) / NSA_ROPE_DIM)
    r = np.arange(LANES) % NSA_HEAD_DIM
    rot = r < NSA_ROPE_DIM
    freq = jnp.where(rot, inv_freq[r % half], 0.0)
    mask_a = ((r >= half) & rot).astype(np.float32)
    mask_b = -(r < half).astype(np.float32)
    rows = jnp.zeros((8, LANES), F32)
    return rows.at[0].set(freq).at[1].set(mask_a).at[2].set(mask_b)


def _nsa_weight_columns():
    kv0 = NSA_QD
    cols = list(range(NSA_QD))
    for kvsel in range(2):
        for branch in range(3):
            base = kv0 + (branch * 2 + kvsel) * NSA_KVD
            cols += list(range(base, base + NSA_KVD))
    gate0 = NSA_QD + 6 * NSA_KVD
    hg = NSA_HEADS // NSA_KV_HEADS
    for g in range(NSA_KV_HEADS):
        lanes = [-1] * LANES
        for branch in range(3):
            for hh in range(hg):
                lanes[branch * 4 + hh] = gate0 + (g * hg + hh) * 3 + branch
        cols += lanes
    return np.asarray(cols, np.int32)


def _group_major(a, b, t):
    return a.reshape(b, t, NSA_KV_HEADS, NSA_HEAD_DIM).transpose(0, 2, 1, 3)


def _pad_pair(k):
    z = jnp.zeros_like(k)
    return jnp.concatenate([k, z], axis=-1), jnp.concatenate([z, k], axis=-1)


def _retention_layer(x2d, b, t, gain, w_in, w_out, cos, sin, tm):
    qk, v, gate = _ret_proj(x2d, gain.reshape(1, D_MODEL), w_in.astype(BF16), cos, sin, tm)
    wide = lambda a: a.reshape(b, t, 2 * D_MODEL)
    o = _retention(wide(qk), wide(v), wide(gate), min(t, 1024))
    return _matmul_res(o.reshape(b * t, RET_HEADS * RET_V_DIM), w_out.astype(BF16), x2d, tm)


def _nsa_layer(x2d, b, t, gain, w_in, cmp_pos, cmp_w1, cmp_w2, w_out, tabs, tm):
    g = NSA_KV_HEADS
    cols = _nsa_weight_columns()
    w_ext = jnp.concatenate([w_in, jnp.zeros((D_MODEL, 1), w_in.dtype)], axis=1)
    w_re = jnp.take(w_ext, jnp.asarray(np.where(cols < 0, w_in.shape[1], cols)), axis=1).astype(BF16)
    qt = min(t, 512)
    kt = min(t, 512)
    q, k_cmp_tok, v_cmp_tok, gates, *slc_win = _nsa_proj(
        x2d, gain.reshape(1, D_MODEL), w_re[:, :NSA_QK], w_re[:, NSA_QK:NSA_QK + NSA_V],
        w_re[:, NSA_QK + NSA_V:], *tabs, tm, t, kt, qt)
    ksa, ksb, kwa, kwb, vse, vso, vwe, vwo = slc_win

    nc = t // CMP_STRIDE
    a2 = jnp.stack([_group_major(k_cmp_tok, b, t), _group_major(v_cmp_tok, b, t)])
    a2 = a2.reshape(2, b, g, nc, CMP_STRIDE * NSA_HEAD_DIM)
    pos8 = jnp.broadcast_to(cmp_pos.reshape(2, 1, CMP_BLOCK * NSA_HEAD_DIM),
                            (2, 8, CMP_BLOCK * NSA_HEAD_DIM)).astype(BF16)
    cmp = _compress(a2, pos8, cmp_w1.astype(BF16), cmp_w2.astype(BF16))
    kca, kcb = _pad_pair(cmp[0])
    vct = cmp[1].transpose(0, 1, 3, 2)
    vc2 = jnp.concatenate([vct, vct], axis=2)

    n_blocks = t // SLC_BLOCK
    kk = np.arange(nc)[None, :]
    nn = np.arange(LANES)[:, None]
    ovl = ((kk * CMP_STRIDE < nn * SLC_BLOCK + SLC_BLOCK) & (kk * CMP_STRIDE + CMP_BLOCK > nn * SLC_BLOCK)
           & (nn < n_blocks) & (kk < nc - 1))
    ovl = jnp.asarray(ovl, BF16)

    attn = _nsa_attn(q.reshape(b, t, NSA_QD), gates.reshape(b, t, NSA_GATE_PAD), ovl, kca, kcb, vc2,
                     ksa, ksb, vse, vso, kwa, kwb, vwe, vwo, qt, kt)
    return _matmul_res(attn.reshape(b * t, NSA_QD), w_out.astype(BF16), x2d, tm)


def _ffn_layer(x2d, gain, w_gu, w_down, tm, final_gain=None):
    fg = None if final_gain is None else final_gain.reshape(1, D_MODEL)
    return _ffn(x2d, gain.reshape(1, D_MODEL), w_gu.astype(BF16), w_down.astype(BF16), tm, final_gain=fg)


def kernel(x, positions, norm_mix, norm_ffn, norm_final, ret_w_in, ret_w_out, nsa_w_in, nsa_cmp_pos,
           nsa_cmp_w1, nsa_cmp_w2, nsa_w_out, ffn_w_gu, ffn_w_down):
    b, t, _ = x.shape
    m = b * t
    tm = min(m, ROW_TILE)
    x2d = x.reshape(m, D_MODEL)
    pos_col = positions.reshape(m, 1).astype(jnp.int32)
    ret_cos, ret_sin, _ = _rope_tables(pos_col, _ret_rope_consts(), tm)
    nsa_tabs = _rope_tables(pos_col, _nsa_rope_consts(), tm)
    for i in range(DEPTH):
        j = i // 2
        if i % 2 == 0:
            x2d = _retention_layer(x2d, b, t, norm_mix[i], ret_w_in[j], ret_w_out[j], ret_cos, ret_sin, tm)
        else:
            x2d = _nsa_layer(x2d, b, t, norm_mix[i], nsa_w_in[j], nsa_cmp_pos[j], nsa_cmp_w1[j],
                             nsa_cmp_w2[j], nsa_w_out[j], nsa_tabs, tm)
        x2d = _ffn_layer(x2d, norm_ffn[i], ffn_w_gu[i], ffn_w_down[i], tm,
                         final_gain=norm_final if i == DEPTH - 1 else None)
    return x2d.reshape(b, t, D_MODEL)
```

```python
import functools

import numpy as np
import jax
import jax.numpy as jnp
from jax import lax
from jax.experimental import pallas as pl
from jax.experimental.pallas import tpu as pltpu

F32 = jnp.float32
BF16 = jnp.bfloat16

D_MODEL = 1024
DEPTH = 4
NORM_EPS = 1e-6
NEG = -1e30

RET_HEADS = 4
RET_QK_DIM = 256
RET_V_DIM = 512
RET_ROPE_THETA = 10000.0
RET_CHUNK = 256

NSA_HEADS = 16
NSA_KV_HEADS = 4
NSA_HEAD_DIM = 64
NSA_ROPE_THETA = 500000.0
NSA_ROPE_DIM = 16
CMP_BLOCK = 32
CMP_STRIDE = 16
CMP_HIDDEN = 256
SLC_BLOCK = 64
SLC_TOPK = 16
WINDOW = 512
FORCE = 1e4
NSA_QD = 1024
NSA_KVD = 256
FFN_HIDDEN = 2816

LANES = 128
SUBLANES = 8
MXU_DIM = 256
ROW_TILE = 512
VMEM_LIMIT = 56 * 1024 * 1024


def _nt_dot(a, b):
    return lax.dot_general(a, b, (((1,), (1,)), ((), ())), preferred_element_type=F32)


def _dot(a, b):
    return jnp.dot(a, b, preferred_element_type=F32)


def _rms_to_bf16(x, gain):
    ms = jnp.mean(x * x, axis=-1, keepdims=True)
    return (x * lax.rsqrt(ms + NORM_EPS) * gain).astype(BF16)


def _resident(shape, index_map):
    return pl.BlockSpec(shape, index_map, pipeline_mode=pl.Buffered(1))


def _rope_tables_kernel(pos_ref, c_ref, cos_ref, sa_ref, sb_ref):
    ang = pos_ref[...].astype(F32) * c_ref[0:1, :]
    s = jnp.sin(ang)
    cos_ref[...] = jnp.cos(ang)
    sa_ref[...] = s * c_ref[1:2, :]
    sb_ref[...] = s * c_ref[2:3, :]


def _rope_tables(pos_col, consts, tm):
    m = pos_col.shape[0]
    out = jax.ShapeDtypeStruct((m, LANES), F32)
    return pl.pallas_call(
        _rope_tables_kernel,
        grid=(m // tm,),
        in_specs=[pl.BlockSpec((tm, 1), lambda i: (i, 0)),
                  pl.BlockSpec((8, LANES), lambda i: (0, 0))],
        out_specs=[pl.BlockSpec((tm, LANES), lambda i: (i, 0))] * 3,
        out_shape=[out, out, out],
        name="rope_tables",
    )(pos_col, consts)


def _ret_proj_kernel(x_ref, gain_ref, wqk_ref, wv_ref, wg_ref, cos_ref, sin_ref, qk_ref, v_ref, g_ref):
    hn = _rms_to_bf16(x_ref[...], gain_ref[...])
    qk = _dot(hn, wqk_ref[...])
    c = cos_ref[...]
    s = sin_ref[...]
    for head in range(2 * RET_HEADS):
        scale = 1.0 if head < RET_HEADS else RET_QK_DIM ** -0.5
        lo = head * RET_QK_DIM
        x1 = qk[:, lo:lo + LANES]
        x2 = qk[:, lo + LANES:lo + 2 * LANES]
        qk_ref[:, lo:lo + LANES] = ((x1 * c - x2 * s) * scale).astype(BF16)
        qk_ref[:, lo + LANES:lo + 2 * LANES] = ((x2 * c + x1 * s) * scale).astype(BF16)
    v_ref[...] = _dot(hn, wv_ref[...]).astype(BF16)
    gate = _dot(hn, wg_ref[...])
    g_ref[...] = (gate * jax.nn.sigmoid(gate)).astype(BF16)


def _ret_proj(x2d, gain, w_bf16, cos, sin, tm):
    m = x2d.shape[0]
    width = 2 * D_MODEL
    out = jax.ShapeDtypeStruct((m, width), BF16)
    return pl.pallas_call(
        _ret_proj_kernel,
        grid=(m // tm,),
        in_specs=[pl.BlockSpec((tm, D_MODEL), lambda i: (i, 0)),
                  _resident((1, D_MODEL), lambda i: (0, 0)),
                  _resident((D_MODEL, width), lambda i: (0, 0)),
                  _resident((D_MODEL, width), lambda i: (0, 1)),
                  _resident((D_MODEL, width), lambda i: (0, 2)),
                  pl.BlockSpec((tm, LANES), lambda i: (i, 0)),
                  pl.BlockSpec((tm, LANES), lambda i: (i, 0))],
        out_specs=[pl.BlockSpec((tm, width), lambda i: (i, 0))] * 3,
        out_shape=[out, out, out],
        compiler_params=pltpu.CompilerParams(dimension_semantics=("arbitrary",),
                                             vmem_limit_bytes=VMEM_LIMIT),
        name="ret_proj",
    )(x2d, gain, w_bf16, w_bf16, w_bf16, cos, sin)


def _retention_kernel(q_ref, k_ref, v_ref, g_ref, o_ref, state, intra, qd, kd, *, chunk, n_chunks):
    c_len = chunk
    h = pl.program_id(1)
    t = pl.program_id(2)

    @pl.when(t == 0)
    def _():
        hv = jnp.full((c_len, RET_QK_DIM), h, jnp.int32)
        den = jnp.left_shift(jnp.full((c_len, RET_QK_DIM), 32, jnp.int32), hv).astype(F32)
        lg = jnp.log(1.0 - 1.0 / den)
        ri = lax.broadcasted_iota(jnp.int32, (c_len, RET_QK_DIM), 0)
        ci = lax.broadcasted_iota(jnp.int32, (c_len, RET_QK_DIM), 1)
        diff = (ri - ci).astype(F32)
        intra[...] = jnp.where(diff >= 0, jnp.exp(lg * jnp.maximum(diff, 0.0)), 0.0)
        rf = ri.astype(F32)
        qd[...] = jnp.exp(lg * (rf + 1.0))
        kd[...] = jnp.exp(lg * (c_len - 1.0 - rf))
        state[...] = jnp.zeros_like(state)

    chunk_decay = qd[c_len - 1:c_len, 0:1]
    for c in range(n_chunks):
        sl = pl.ds(c * c_len, c_len)
        qc = q_ref[0, sl, :]
        kc = k_ref[0, sl, :]
        vc = v_ref[0, sl, :]
        s = _nt_dot(qc, kc) * intra[...]
        st = state[...]
        qdec = (qc.astype(F32) * qd[...]).astype(BF16)
        o = _dot(s.astype(BF16), vc) + _dot(qdec, st.astype(BF16))
        kdec = (kc.astype(F32) * kd[...]).astype(BF16)
        state[...] = st * chunk_decay + lax.dot_general(
            kdec, vc, (((0,), (0,)), ((), ())), preferred_element_type=F32)
        ms = jnp.mean(o * o, axis=-1, keepdims=True)
        y = o * lax.rsqrt(ms + NORM_EPS)
        o_ref[0, sl, :] = (y * g_ref[0, sl, :].astype(F32)).astype(BF16)


def _retention(qk3, v3, g3, tq):
    b, t, _ = qk3.shape
    chunk = RET_CHUNK
    assert RET_CHUNK == RET_QK_DIM and tq % chunk == 0 and t % tq == 0
    kern = functools.partial(_retention_kernel, chunk=chunk, n_chunks=tq // chunk)
    nq = D_MODEL // RET_QK_DIM
    return pl.pallas_call(
        kern,
        grid=(b, RET_HEADS, t // tq),
        in_specs=[pl.BlockSpec((1, tq, RET_QK_DIM), lambda bb, h, tt: (bb, tt, h)),
                  pl.BlockSpec((1, tq, RET_QK_DIM), lambda bb, h, tt: (bb, tt, nq + h)),
                  pl.BlockSpec((1, tq, RET_V_DIM), lambda bb, h, tt: (bb, tt, h)),
                  pl.BlockSpec((1, tq, RET_V_DIM), lambda bb, h, tt: (bb, tt, h))],
        out_specs=pl.BlockSpec((1, tq, RET_V_DIM), lambda bb, h, tt: (bb, tt, h)),
        out_shape=jax.ShapeDtypeStruct((b, t, RET_HEADS * RET_V_DIM), BF16),
        scratch_shapes=[pltpu.VMEM((RET_QK_DIM, RET_V_DIM), F32),
                        pltpu.VMEM((chunk, RET_QK_DIM), F32),
                        pltpu.VMEM((chunk, RET_QK_DIM), F32),
                        pltpu.VMEM((chunk, RET_QK_DIM), F32)],
        compiler_params=pltpu.CompilerParams(dimension_semantics=("arbitrary",) * 3),
        name="retention",
    )(qk3, qk3, v3, g3)


def _matmul_res_kernel(a_ref, w_ref, res_ref, o_ref):
    o_ref[...] = res_ref[...] + _dot(a_ref[...], w_ref[...])


def _matmul_res(a, w_bf16, res, tm):
    m, k = a.shape
    n = w_bf16.shape[1]
    return pl.pallas_call(
        _matmul_res_kernel,
        grid=(m // tm,),
        in_specs=[pl.BlockSpec((tm, k), lambda i: (i, 0)),
                  _resident((k, n), lambda i: (0, 0)),
                  pl.BlockSpec((tm, n), lambda i: (i, 0))],
        out_specs=pl.BlockSpec((tm, n), lambda i: (i, 0)),
        out_shape=jax.ShapeDtypeStruct((m, n), F32),
        compiler_params=pltpu.CompilerParams(dimension_semantics=("arbitrary",)),
        name="matmul_res",
    )(a, w_bf16, res)


def _ffn_kernel(x_ref, gain_ref, wg_ref, wu_ref, wd_ref, *rest, final_norm):
    if final_norm:
        fgain_ref, o_ref = rest
    else:
        (o_ref,) = rest
    x = x_ref[...]
    hn = _rms_to_bf16(x, gain_ref[...])
    a = _dot(hn, wg_ref[...])
    b = _dot(hn, wu_ref[...])
    hid = (a * jax.nn.sigmoid(a) * b).astype(BF16)
    y = x + _dot(hid, wd_ref[...])
    if final_norm:
        ms = jnp.mean(y * y, axis=-1, keepdims=True)
        y = y * lax.rsqrt(ms + NORM_EPS) * fgain_ref[...]
    o_ref[...] = y


def _ffn(x2d, gain, w_gu_bf16, w_down_bf16, tm, final_gain=None):
    m = x2d.shape[0]
    in_specs = [pl.BlockSpec((tm, D_MODEL), lambda i: (i, 0)),
                _resident((1, D_MODEL), lambda i: (0, 0)),
                _resident((D_MODEL, FFN_HIDDEN), lambda i: (0, 0)),
                _resident((D_MODEL, FFN_HIDDEN), lambda i: (0, 1)),
                _resident((FFN_HIDDEN, D_MODEL), lambda i: (0, 0))]
    args = [x2d, gain, w_gu_bf16, w_gu_bf16, w_down_bf16]
    if final_gain is not None:
        in_specs.append(_resident((1, D_MODEL), lambda i: (0, 0)))
        args.append(final_gain)
    return pl.pallas_call(
        functools.partial(_ffn_kernel, final_norm=final_gain is not None),
        grid=(m // tm,),
        in_specs=in_specs,
        out_specs=pl.BlockSpec((tm, D_MODEL), lambda i: (i, 0)),
        out_shape=jax.ShapeDtypeStruct((m, D_MODEL), F32),
        compiler_params=pltpu.CompilerParams(dimension_semantics=("arbitrary",),
                                             vmem_limit_bytes=VMEM_LIMIT),
        name="ffn",
    )(*args)


NSA_QK = NSA_QD + 3 * NSA_KVD
NSA_V = 3 * NSA_KVD
NSA_GATE_PAD = NSA_KV_HEADS * LANES
Q_SCALE_LOG2 = NSA_HEAD_DIM ** -0.5 * float(np.log2(np.e))


def _partial_rotary(a, c, sa, sb):
    return a * c + pltpu.roll(a, 8, 1) * sa + pltpu.roll(a, LANES - 8, 1) * sb


def _t_rows(x):
    n = x.shape[0] // LANES
    return jnp.concatenate([x[c * LANES:(c + 1) * LANES, :].T for c in range(n)], axis=1)


def _nsa_proj_kernel(x_ref, gain_ref, wqk_ref, wv_ref, wg_ref, c_ref, sa_ref, sb_ref,
                     q_ref, kc_ref, vc_ref, gate_ref, ksa_ref, ksb_ref, kwa_ref, kwb_ref,
                     vse_ref, vso_ref, vwe_ref, vwo_ref, *, t, kt, qt):
    tm = x_ref.shape[0]
    half = LANES // 2
    hn = _rms_to_bf16(x_ref[...], gain_ref[...])
    qk = _dot(hn, wqk_ref[...])
    c = c_ref[...]
    sa = sa_ref[...]
    sb = sb_ref[...]

    def rotated(lo, scale=1.0):
        return _partial_rotary(qk[:, lo:lo + LANES], c, sa, sb) * scale

    for chunk in range(NSA_QD // LANES):
        q_ref[:, chunk * LANES:(chunk + 1) * LANES] = rotated(chunk * LANES, Q_SCALE_LOG2).astype(BF16)
    for chunk in range(NSA_KVD // LANES):
        kc_ref[:, chunk * LANES:(chunk + 1) * LANES] = rotated(NSA_QD + chunk * LANES).astype(BF16)

    lane = lax.broadcasted_iota(jnp.int32, (tm, LANES), 1)
    low = lane < half
    pos = lax.rem(pl.program_id(0) * tm + lax.broadcasted_iota(jnp.int32, (tm, LANES), 0), t)
    onehot = jnp.where((lane & (half - 1)) == lax.shift_right_logical(pos, 6), 1.0, 0.0)
    zero = jnp.zeros((tm, LANES), F32)
    for branch, (a_ref, b_ref, fill) in enumerate(((ksa_ref, ksb_ref, onehot), (kwa_ref, kwb_ref, zero))):
        base = NSA_QD + (1 + branch) * NSA_KVD
        for chunk in range(NSA_KVD // LANES):
            kk = rotated(base + chunk * LANES)
            swapped = pltpu.roll(kk, half, 1)
            a_ref[2 * chunk] = jnp.where(low, kk, fill).astype(BF16)
            b_ref[2 * chunk] = jnp.where(low, fill, swapped).astype(BF16)
            a_ref[2 * chunk + 1] = jnp.where(low, swapped, fill).astype(BF16)
            b_ref[2 * chunk + 1] = jnp.where(low, fill, kk).astype(BF16)

    v = _dot(hn, wv_ref[...])
    vc_ref[...] = v[:, :NSA_KVD].astype(BF16)
    ones = jnp.ones((half, tm), F32)
    for branch, (e_ref, o_ref, tile) in enumerate(((vse_ref, vso_ref, kt), (vwe_ref, vwo_ref, qt))):
        base = (1 + branch) * NSA_KVD
        for chunk in range(NSA_KVD // LANES):
            v_t = _t_rows(v[:, base + chunk * LANES:base + (chunk + 1) * LANES])
            for par in range(2):
                vg = v_t[par * half:(par + 1) * half]
                even = jnp.concatenate([vg, ones], axis=0).astype(BF16)
                odd = jnp.concatenate([ones, vg], axis=0).astype(BF16)
                for j in range(tm // tile):
                    e_ref[2 * chunk + par, j] = even[:, j * tile:(j + 1) * tile]
                    o_ref[2 * chunk + par, j] = odd[:, j * tile:(j + 1) * tile]
    gate_ref[...] = jax.nn.sigmoid(_dot(hn, wg_ref[...]))


def _nsa_proj(x2d, gain, wqk, wv, wg, c, sa, sb, tm, t, kt, qt):
    m = x2d.shape[0]
    g = NSA_KV_HEADS
    assert SLC_BLOCK == 64 and t % tm == 0 and tm % kt == 0 and tm % qt == 0
    row = lambda i: (i, 0)
    fixed = lambda i: (0, 0)
    key_spec = pl.BlockSpec((g, tm, LANES), lambda i: (0, i, 0))
    key_shape = jax.ShapeDtypeStruct((g, m, LANES), BF16)

    def val(tile):
        return (pl.BlockSpec((g, tm // tile, LANES, tile), lambda i: (0, i, 0, 0)),
                jax.ShapeDtypeStruct((g, m // tile, LANES, tile), BF16))

    return pl.pallas_call(
        functools.partial(_nsa_proj_kernel, t=t, kt=kt, qt=qt),
        grid=(m // tm,),
        in_specs=[pl.BlockSpec((tm, D_MODEL), row),
                  _resident((1, D_MODEL), fixed),
                  _resident((D_MODEL, NSA_QK), fixed),
                  _resident((D_MODEL, NSA_V), fixed),
                  _resident((D_MODEL, NSA_GATE_PAD), fixed),
                  pl.BlockSpec((tm, LANES), row),
                  pl.BlockSpec((tm, LANES), row),
                  pl.BlockSpec((tm, LANES), row)],
        out_specs=[pl.BlockSpec((tm, NSA_QD), row),
                   pl.BlockSpec((tm, NSA_KVD), row),
                   pl.BlockSpec((tm, NSA_KVD), row),
                   pl.BlockSpec((tm, NSA_GATE_PAD), row),
                   key_spec, key_spec, key_spec, key_spec,
                   val(kt)[0], val(kt)[0], val(qt)[0], val(qt)[0]],
        out_shape=[jax.ShapeDtypeStruct((m, NSA_QD), BF16),
                   jax.ShapeDtypeStruct((m, NSA_KVD), BF16),
                   jax.ShapeDtypeStruct((m, NSA_KVD), BF16),
                   jax.ShapeDtypeStruct((m, NSA_GATE_PAD), F32),
                   key_shape, key_shape, key_shape, key_shape,
                   val(kt)[1], val(kt)[1], val(qt)[1], val(qt)[1]],
        compiler_params=pltpu.CompilerParams(dimension_semantics=("arbitrary",),
                                             vmem_limit_bytes=VMEM_LIMIT),
        name="nsa_proj",
    )(x2d, gain, wqk, wv, wg, c, sa, sb)


def _compress_kernel(a_ref, pos_ref, w1_ref, w2_ref, o_ref):
    a = a_ref[0, 0, 0]
    nc = a.shape[0]
    half = CMP_STRIDE * NSA_HEAD_DIM
    upper = _dot(a, w1_ref[0, :half, :])
    lower = _dot(a, w1_ref[0, half:, :])
    cpos = _dot(pos_ref[0], w1_ref[0])
    hid = upper + pltpu.roll(lower, nc - 1, 0) + cpos[0:1, :]
    hid = hid * jax.nn.sigmoid(hid)
    o_ref[0, 0, 0] = _dot(hid.astype(BF16), w2_ref[0]).astype(BF16)


def _compress(a2, pos8, w1, w2):
    _, b, g, nc, width = a2.shape
    return pl.pallas_call(
        _compress_kernel,
        grid=(2, b, g),
        in_specs=[pl.BlockSpec((1, 1, 1, nc, width), lambda s, bb, gg: (s, bb, gg, 0, 0)),
                  pl.BlockSpec((1, 8, width * 2), lambda s, bb, gg: (s, 0, 0)),
                  pl.BlockSpec((1, width * 2, CMP_HIDDEN), lambda s, bb, gg: (s, 0, 0)),
                  pl.BlockSpec((1, CMP_HIDDEN, NSA_HEAD_DIM), lambda s, bb, gg: (s, 0, 0))],
        out_specs=pl.BlockSpec((1, 1, 1, nc, NSA_HEAD_DIM), lambda s, bb, gg: (s, bb, gg, 0, 0)),
        out_shape=jax.ShapeDtypeStruct((2, b, g, nc, NSA_HEAD_DIM), BF16),
        compiler_params=pltpu.CompilerParams(dimension_semantics=("arbitrary",) * 3),
        name="nsa_compress",
    )(a2, pos8, w1, w2)


def _t_cols(x):
    n = x.shape[1] // LANES
    return jnp.concatenate([x[:, c * LANES:(c + 1) * LANES].T for c in range(n)], axis=0)


def _flash_update(par, st, v_aug, m_sc, acc_sc):
    m_old = m_sc[par]
    m_new = jnp.maximum(m_old, jnp.max(st, axis=0, keepdims=True))
    p = jnp.exp2(st - m_new)
    acc_sc[par] = acc_sc[par] * jnp.exp2(m_old - m_new) + _dot(v_aug, p.astype(BF16))
    m_sc[par] = m_new


def _normalised(acc, par):
    half = LANES // 2
    if par == 0:
        return acc[:half] * (1.0 / acc[half:half + 1])
    return acc[half:] * (1.0 / acc[0:1])


def _nsa_attn_kernel(q_ref, gate_ref, ovl_ref, kca_ref, kcb_ref, vc_ref,
                     ksa_ref, ksb_ref, vse_ref, vso_ref, kwa_ref, kwb_ref, vwe_ref, vwo_ref,
                     *rest, qi, qt, kt, n_sel, n_back):
    o_ref, qm_sc, qaug_sc, x_sc, tot_sc, m_sc, acc_sc, sta_sc, stb_sc = rest[-9:]
    t0 = qi * qt
    half = LANES // 2
    wide = 2 * qt
    lane = lax.broadcasted_iota(jnp.int32, (qt, LANES), 1)
    low = lane < NSA_HEAD_DIM
    zero = jnp.zeros((qt, LANES), BF16)
    qps = [q_ref[0, :, pair * LANES:(pair + 1) * LANES] for pair in range(2)]
    for pair in range(2):
        rows = slice(pair * qt, (pair + 1) * qt)
        qm_sc[0, rows] = jnp.where(low, qps[pair], zero)
        qm_sc[1, rows] = jnp.where(low, zero, qps[pair])

    gates_t = _t_rows(gate_ref[0])

    def gate_row(branch, par):
        return jnp.concatenate([gates_t[branch * 4 + 2 * pair + par:branch * 4 + 2 * pair + par + 1, :]
                                for pair in range(2)], axis=1)

    col = lax.broadcasted_iota(jnp.int32, (1, wide), 1)
    tcol = t0 + jnp.where(col >= qt, col - qt, col)

    t_last = t0 + qt - 1
    nc = min(kca_ref.shape[2], -(-((t_last - CMP_BLOCK + 1) // CMP_STRIDE + 1) // LANES) * LANES)
    kidx = lax.broadcasted_iota(jnp.int32, (nc, wide), 0)
    valid = kidx * CMP_STRIDE + (CMP_BLOCK - 1) <= tcol
    vct = vc_ref[0, 0, :, :nc]
    has_block = tcol >= CMP_BLOCK - 1
    psum = None
    for par in range(2):
        st = _nt_dot((kcb_ref if par else kca_ref)[0, 0, :nc, :], qm_sc[par])
        st = jnp.where(valid, st, NEG)
        m = jnp.max(st, axis=0, keepdims=True)
        e = jnp.exp2(st - m)
        l = jnp.sum(e, axis=0, keepdims=True)
        p = e * jnp.where(has_block, 1.0 / l, 0.0)
        ph = p[:, :qt] + p[:, qt:]
        psum = ph if psum is None else psum + ph
        o_t = _dot(vct, p.astype(BF16))
        tot_sc[par] = o_t[:half] * gate_row(0, par)

    ovl = ovl_ref[:, :nc]
    hi = psum.astype(BF16)
    r1 = psum - hi.astype(F32)
    mid = r1.astype(BF16)
    lo = (r1 - mid.astype(F32)).astype(BF16)
    imp = (_dot(ovl, hi) + _dot(ovl, mid) + _dot(ovl, lo))[:half, :]
    nio = lax.broadcasted_iota(jnp.int32, (half, qt), 0)
    cur = lax.shift_right_logical(t0 + lax.broadcasted_iota(jnp.int32, (half, qt), 1), 6)
    forced = (nio == 0) | (nio == cur) | (nio == cur - 1)
    x_sc[...] = jnp.where(nio > cur, NEG, jnp.where(forced, imp + FORCE, imp))

    sub = lax.broadcasted_iota(jnp.int32, (SUBLANES, qt), 0)
    n_live = (t_last // SLC_BLOCK) // SUBLANES + 1
    xs = [x_sc[r * SUBLANES:(r + 1) * SUBLANES, :] for r in range(n_live)]
    cnts = [jnp.zeros((SUBLANES, qt), jnp.int32) for _ in range(n_live)]
    for rm in range(n_live):
        for mm in range(SUBLANES):
            row = xs[rm][mm:mm + 1, :]
            for r in range(n_live):
                if r < rm:
                    beats = row > xs[r]
                elif r > rm:
                    beats = row >= xs[r]
                else:
                    beats = (row > xs[r]) | ((row == xs[r]) & (sub > mm))
                cnts[r] = cnts[r] + jnp.where(beats, 1, 0)
    bias_t = jnp.where(jnp.concatenate(cnts, axis=0) < n_sel, 0.0, NEG).astype(F32)
    if n_live * SUBLANES < half:
        bias_t = jnp.concatenate([bias_t, jnp.full((half - n_live * SUBLANES, qt), NEG, F32)], axis=0)
    bias = _t_cols(jnp.concatenate([bias_t, bias_t], axis=0)).astype(BF16)
    for pair in range(2):
        rows = slice(pair * qt, (pair + 1) * qt)
        qaug_sc[0, rows] = jnp.where(low, qps[pair], bias)
        qaug_sc[1, rows] = jnp.where(low, bias, qps[pair])

    ri = lax.broadcasted_iota(jnp.int32, (qt, wide), 0)
    ci = lax.broadcasted_iota(jnp.int32, (qt, wide), 1)
    ci = jnp.where(ci >= qt, ci - qt, ci)
    backs = [back for back in range(n_back + 1) if qi >= back]
    for par in range(2):
        k_ref = kwb_ref if par else kwa_ref
        v_ref = vwo_ref if par else vwe_ref
        sts = []
        for back in backs:
            st = _nt_dot(k_ref[0, (qi - back) * qt:(qi - back + 1) * qt, :], qm_sc[par])
            if back == 0:
                st = jnp.where(ri <= ci, st, NEG)
            elif back == n_back:
                st = jnp.where(ri > ci, st, NEG)
            sts.append(st)
        m = functools.reduce(jnp.maximum, [jnp.max(st, axis=0, keepdims=True) for st in sts])
        acc = None
        for back, st in zip(backs, sts):
            part = _dot(v_ref[0, qi - back], jnp.exp2(st - m).astype(BF16))
            acc = part if acc is None else acc + part
        tot_sc[par] = tot_sc[par] + _normalised(acc, par) * gate_row(2, par)

    m_sc[...] = jnp.full(m_sc.shape, NEG, F32)
    acc_sc[...] = jnp.zeros_like(acc_sc)
    bufs = (sta_sc, stb_sc)

    def scores(kidx):
        buf = bufs[kidx % 2]
        buf[0] = _nt_dot(ksa_ref[0, kidx * kt:(kidx + 1) * kt, :], qaug_sc[0])
        buf[1] = _nt_dot(ksb_ref[0, kidx * kt:(kidx + 1) * kt, :], qaug_sc[1])

    def consume(kidx, masked):
        buf = bufs[kidx % 2]
        v_aug = (vse_ref[0, kidx], vso_ref[0, kidx])
        if masked:
            keep = kidx * kt + lax.broadcasted_iota(jnp.int32, (kt, wide), 0) <= tcol
        for par in range(2):
            st = buf[par]
            if masked:
                st = jnp.where(keep, st, NEG)
            _flash_update(par, st, v_aug[par], m_sc, acc_sc)

    assert qt <= kt
    last = t_last // kt
    scores(0)
    for kidx in range(last):
        scores(kidx + 1)
        consume(kidx, False)
    consume(last, True)

    total = [tot_sc[par] + _normalised(acc_sc[par], par) * gate_row(1, par) for par in range(2)]
    for pair in range(2):
        cols = slice(pair * qt, (pair + 1) * qt)
        pair_t = jnp.concatenate([total[0][:, cols], total[1][:, cols]], axis=0)
        o_ref[0, :, pair * LANES:(pair + 1) * LANES] = _t_cols(pair_t).astype(BF16)


def _nsa_attn(qk3, gates3, ovl, kca, kcb, vc2, ksa, ksb, vse, vso, kwa, kwb, vwe, vwo, qt, kt):
    b, t, _ = qk3.shape
    g = NSA_KV_HEADS
    nc = kca.shape[2]
    n_blocks = t // SLC_BLOCK
    assert n_blocks <= LANES // 2 and SLC_BLOCK == 64 and WINDOW % qt == 0
    per_group = lambda *tail: (lambda bb, gg: (bb, gg) + tail)
    kc_spec = pl.BlockSpec((1, 1, nc, LANES), per_group(0, 0))
    k_spec = pl.BlockSpec((1, t, LANES), lambda bb, gg: (gg, bb, 0))
    vs_spec = pl.BlockSpec((1, t // kt, LANES, kt), lambda bb, gg: (gg, bb, 0, 0))
    vw_spec = pl.BlockSpec((1, t // qt, LANES, qt), lambda bb, gg: (gg, bb, 0, 0))
    half = LANES // 2
    args = [qk3, gates3, ovl, kca, kcb, vc2, ksa, ksb, vse, vso, kwa, kwb, vwe, vwo]
    out = None
    for qi in range(t // qt):
        tile_map = lambda bb, gg, qi=qi: (bb, qi, gg)
        in_specs = [pl.BlockSpec((1, qt, MXU_DIM), tile_map),
                    pl.BlockSpec((1, qt, LANES), tile_map),
                    pl.BlockSpec((LANES, nc), lambda bb, gg: (0, 0)),
                    kc_spec, kc_spec,
                    pl.BlockSpec((1, 1, LANES, nc), per_group(0, 0)),
                    k_spec, k_spec, vs_spec, vs_spec,
                    k_spec, k_spec, vw_spec, vw_spec]
        if out is not None:
            in_specs.append(pl.BlockSpec(memory_space=pl.ANY))
        out = pl.pallas_call(
            functools.partial(_nsa_attn_kernel, qi=qi, qt=qt, kt=kt,
                              n_sel=min(SLC_TOPK, n_blocks), n_back=WINDOW // qt),
            grid=(b, g),
            in_specs=in_specs,
            out_specs=pl.BlockSpec((1, qt, MXU_DIM), tile_map),
            out_shape=jax.ShapeDtypeStruct((b, t, NSA_QD), BF16),
            scratch_shapes=[pltpu.VMEM((2, 2 * qt, LANES), BF16),
                            pltpu.VMEM((2, 2 * qt, LANES), BF16),
                            pltpu.VMEM((half, qt), F32),
                            pltpu.VMEM((2, half, 2 * qt), F32),
                            pltpu.VMEM((2, 1, 2 * qt), F32),
                            pltpu.VMEM((2, LANES, 2 * qt), F32),
                            pltpu.VMEM((2, kt, 2 * qt), F32),
                            pltpu.VMEM((2, kt, 2 * qt), F32)],
            input_output_aliases={} if out is None else {len(args): 0},
            compiler_params=pltpu.CompilerParams(dimension_semantics=("arbitrary",) * 2,
                                                 vmem_limit_bytes=VMEM_LIMIT),
            name=f"nsa_attn_q{qi}",
        )(*(args if out is None else args + [out]))
    return out


def _ret_rope_consts():
    half = RET_QK_DIM // 2
    inv_freq = RET_ROPE_THETA ** (-2.0 * jnp.arange(half, dtype=F32) / RET_QK_DIM)
    rows = jnp.zeros((8, LANES), F32)
    return rows.at[0].set(inv_freq).at[1].set(1.0)


def _nsa_rope_consts():
    half = NSA_ROPE_DIM // 2
    inv_freq = NSA_ROPE_THETA ** (-2.0 * jnp.arange(half, dtype=F32) / NSA_ROPE_DIM)
    r = np.arange(LANES) % NSA_HEAD_DIM
    rot = r < NSA_ROPE_DIM
    freq = jnp.where(rot, inv_freq[r % half], 0.0)
    mask_a = ((r >= half) & rot).astype(np.float32)
    mask_b = -(r < half).astype(np.float32)
    rows = jnp.zeros((8, LANES), F32)
    return rows.at[0].set(freq).at[1].set(mask_a).at[2].set(mask_b)


def _nsa_weight_columns():
    kv0 = NSA_QD
    cols = list(range(NSA_QD))
    for kvsel in range(2):
        for branch in range(3):
            base = kv0 + (branch * 2 + kvsel) * NSA_KVD
            cols += list(range(base, base + NSA_KVD))
    gate0 = NSA_QD + 6 * NSA_KVD
    hg = NSA_HEADS // NSA_KV_HEADS
    for g in range(NSA_KV_HEADS):
        lanes = [-1] * LANES
        for branch in range(3):
            for hh in range(hg):
                lanes[branch * 4 + hh] = gate0 + (g * hg + hh) * 3 + branch
        cols += lanes
    return np.asarray(cols, np.int32)


def _group_major(a, b, t):
    return a.reshape(b, t, NSA_KV_HEADS, NSA_HEAD_DIM).transpose(0, 2, 1, 3)


def _pad_pair(k):
    z = jnp.zeros_like(k)
    return jnp.concatenate([k, z], axis=-1), jnp.concatenate([z, k], axis=-1)


def _retention_layer(x2d, b, t, gain, w_in, w_out, cos, sin, tm):
    qk, v, gate = _ret_proj(x2d, gain.reshape(1, D_MODEL), w_in.astype(BF16), cos, sin, tm)
    wide = lambda a: a.reshape(b, t, 2 * D_MODEL)
    o = _retention(wide(qk), wide(v), wide(gate), min(t, 1024))
    return _matmul_res(o.reshape(b * t, RET_HEADS * RET_V_DIM), w_out.astype(BF16), x2d, tm)


def _nsa_layer(x2d, b, t, gain, w_in, cmp_pos, cmp_w1, cmp_w2, w_out, tabs, tm):
    g = NSA_KV_HEADS
    cols = _nsa_weight_columns()
    w_ext = jnp.concatenate([w_in, jnp.zeros((D_MODEL, 1), w_in.dtype)], axis=1)
    w_re = jnp.take(w_ext, jnp.asarray(np.where(cols < 0, w_in.shape[1], cols)), axis=1).astype(BF16)
    qt = min(t, 512)
    kt = min(t, 512)
    q, k_cmp_tok, v_cmp_tok, gates, *slc_win = _nsa_proj(
        x2d, gain.reshape(1, D_MODEL), w_re[:, :NSA_QK], w_re[:, NSA_QK:NSA_QK + NSA_V],
        w_re[:, NSA_QK + NSA_V:], *tabs, tm, t, kt, qt)
    ksa, ksb, kwa, kwb, vse, vso, vwe, vwo = slc_win

    nc = t // CMP_STRIDE
    a2 = jnp.stack([_group_major(k_cmp_tok, b, t), _group_major(v_cmp_tok, b, t)])
    a2 = a2.reshape(2, b, g, nc, CMP_STRIDE * NSA_HEAD_DIM)
    pos8 = jnp.broadcast_to(cmp_pos.reshape(2, 1, CMP_BLOCK * NSA_HEAD_DIM),
                            (2, 8, CMP_BLOCK * NSA_HEAD_DIM)).astype(BF16)
    cmp = _compress(a2, pos8, cmp_w1.astype(BF16), cmp_w2.astype(BF16))
    kca, kcb = _pad_pair(cmp[0])
    vct = cmp[1].transpose(0, 1, 3, 2)
    vc2 = jnp.concatenate([vct, vct], axis=2)

    n_blocks = t // SLC_BLOCK
    kk = np.arange(nc)[None, :]
    nn = np.arange(LANES)[:, None]
    ovl = ((kk * CMP_STRIDE < nn * SLC_BLOCK + SLC_BLOCK) & (kk * CMP_STRIDE + CMP_BLOCK > nn * SLC_BLOCK)
           & (nn < n_blocks) & (kk < nc - 1))
    ovl = jnp.asarray(ovl, BF16)

    attn = _nsa_attn(q.reshape(b, t, NSA_QD), gates.reshape(b, t, NSA_GATE_PAD), ovl, kca, kcb, vc2,
                     ksa, ksb, vse, vso, kwa, kwb, vwe, vwo, qt, kt)
    return _matmul_res(attn.reshape(b * t, NSA_QD), w_out.astype(BF16), x2d, tm)


def _ffn_layer(x2d, gain, w_gu, w_down, tm, final_gain=None):
    fg = None if final_gain is None else final_gain.reshape(1, D_MODEL)
    return _ffn(x2d, gain.reshape(1, D_MODEL), w_gu.astype(BF16), w_down.astype(BF16), tm, final_gain=fg)


def kernel(x, positions, norm_mix, norm_ffn, norm_final, ret_w_in, ret_w_out, nsa_w_in, nsa_cmp_pos,
           nsa_cmp_w1, nsa_cmp_w2, nsa_w_out, ffn_w_gu, ffn_w_down):
    b, t, _ = x.shape
    m = b * t
    tm = min(m, ROW_TILE)
    x2d = x.reshape(m, D_MODEL)
    pos_col = positions.reshape(m, 1).astype(jnp.int32)
    ret_cos, ret_sin, _ = _rope_tables(pos_col, _ret_rope_consts(), tm)
    nsa_tabs = _rope_tables(pos_col, _nsa_rope_consts(), tm)
    for i in range(DEPTH):
        j = i // 2
        if i % 2 == 0:
            x2d = _retention_layer(x2d, b, t, norm_mix[i], ret_w_in[j], ret_w_out[j], ret_cos, ret_sin, tm)
        else:
            x2d = _nsa_layer(x2d, b, t, norm_mix[i], nsa_w_in[j], nsa_cmp_pos[j], nsa_cmp_w1[j],
                             nsa_cmp_w2[j], nsa_w_out[j], nsa_tabs, tm)
        x2d = _ffn_layer(x2d, norm_ffn[i], ffn_w_gu[i], ffn_w_down[i], tm,
                         final_gain=norm_final if i == DEPTH - 1 else None)
    return x2d.reshape(b, t, D_MODEL)
```

```python
import functools

import numpy as np
import jax
import jax.numpy as jnp
from jax import lax
from jax.experimental import pallas as pl
from jax.experimental.pallas import tpu as pltpu

F32 = jnp.float32
BF16 = jnp.bfloat16

D_MODEL = 1024
DEPTH = 4
NORM_EPS = 1e-6
NEG = -1e30

RET_HEADS = 4
RET_QK_DIM = 256
RET_V_DIM = 512
RET_ROPE_THETA = 10000.0
RET_CHUNK = 256

NSA_HEADS = 16
NSA_KV_HEADS = 4
NSA_HEAD_DIM = 64
NSA_ROPE_THETA = 500000.0
NSA_ROPE_DIM = 16
CMP_BLOCK = 32
CMP_STRIDE = 16
CMP_HIDDEN = 256
SLC_BLOCK = 64
SLC_TOPK = 16
WINDOW = 512
FORCE = 1e4
NSA_QD = 1024
NSA_KVD = 256
FFN_HIDDEN = 2816

LANES = 128
SUBLANES = 8
MXU_DIM = 256
ROW_TILE = 512
VMEM_LIMIT = 56 * 1024 * 1024


def _nt_dot(a, b):
    return lax.dot_general(a, b, (((1,), (1,)), ((), ())), preferred_element_type=F32)


def _dot(a, b):
    return jnp.dot(a, b, preferred_element_type=F32)


def _rms_to_bf16(x, gain):
    ms = jnp.mean(x * x, axis=-1, keepdims=True)
    return (x * lax.rsqrt(ms + NORM_EPS) * gain).astype(BF16)


def _resident(shape, index_map):
    return pl.BlockSpec(shape, index_map, pipeline_mode=pl.Buffered(1))


def _rope_tables_kernel(pos_ref, c_ref, cos_ref, sa_ref, sb_ref):
    ang = pos_ref[...].astype(F32) * c_ref[0:1, :]
    s = jnp.sin(ang)
    cos_ref[...] = jnp.cos(ang)
    sa_ref[...] = s * c_ref[1:2, :]
    sb_ref[...] = s * c_ref[2:3, :]


def _rope_tables(pos_col, consts, tm):
    m = pos_col.shape[0]
    out = jax.ShapeDtypeStruct((m, LANES), F32)
    return pl.pallas_call(
        _rope_tables_kernel,
        grid=(m // tm,),
        in_specs=[pl.BlockSpec((tm, 1), lambda i: (i, 0)),
                  pl.BlockSpec((8, LANES), lambda i: (0, 0))],
        out_specs=[pl.BlockSpec((tm, LANES), lambda i: (i, 0))] * 3,
        out_shape=[out, out, out],
        name="rope_tables",
    )(pos_col, consts)


def _ret_proj_kernel(x_ref, gain_ref, wqk_ref, wv_ref, wg_ref, cos_ref, sin_ref, qk_ref, v_ref, g_ref):
    hn = _rms_to_bf16(x_ref[...], gain_ref[...])
    qk = _dot(hn, wqk_ref[...])
    c = cos_ref[...]
    s = sin_ref[...]
    for head in range(2 * RET_HEADS):
        scale = 1.0 if head < RET_HEADS else RET_QK_DIM ** -0.5
        lo = head * RET_QK_DIM
        x1 = qk[:, lo:lo + LANES]
        x2 = qk[:, lo + LANES:lo + 2 * LANES]
        qk_ref[:, lo:lo + LANES] = ((x1 * c - x2 * s) * scale).astype(BF16)
        qk_ref[:, lo + LANES:lo + 2 * LANES] = ((x2 * c + x1 * s) * scale).astype(BF16)
    v_ref[...] = _dot(hn, wv_ref[...]).astype(BF16)
    gate = _dot(hn, wg_ref[...])
    g_ref[...] = (gate * jax.nn.sigmoid(gate)).astype(BF16)


def _ret_proj(x2d, gain, w_bf16, cos, sin, tm):
    m = x2d.shape[0]
    width = 2 * D_MODEL
    out = jax.ShapeDtypeStruct((m, width), BF16)
    return pl.pallas_call(
        _ret_proj_kernel,
        grid=(m // tm,),
        in_specs=[pl.BlockSpec((tm, D_MODEL), lambda i: (i, 0)),
                  _resident((1, D_MODEL), lambda i: (0, 0)),
                  _resident((D_MODEL, width), lambda i: (0, 0)),
                  _resident((D_MODEL, width), lambda i: (0, 1)),
                  _resident((D_MODEL, width), lambda i: (0, 2)),
                  pl.BlockSpec((tm, LANES), lambda i: (i, 0)),
                  pl.BlockSpec((tm, LANES), lambda i: (i, 0))],
        out_specs=[pl.BlockSpec((tm, width), lambda i: (i, 0))] * 3,
        out_shape=[out, out, out],
        compiler_params=pltpu.CompilerParams(dimension_semantics=("arbitrary",),
                                             vmem_limit_bytes=VMEM_LIMIT),
        name="ret_proj",
    )(x2d, gain, w_bf16, w_bf16, w_bf16, cos, sin)


def _retention_kernel(q_ref, k_ref, v_ref, g_ref, o_ref, state, intra, qd, kd, *, chunk, n_chunks):
    c_len = chunk
    h = pl.program_id(1)
    t = pl.program_id(2)

    @pl.when(t == 0)
    def _():
        hv = jnp.full((c_len, RET_QK_DIM), h, jnp.int32)
        den = jnp.left_shift(jnp.full((c_len, RET_QK_DIM), 32, jnp.int32), hv).astype(F32)
        lg = jnp.log(1.0 - 1.0 / den)
        ri = lax.broadcasted_iota(jnp.int32, (c_len, RET_QK_DIM), 0)
        ci = lax.broadcasted_iota(jnp.int32, (c_len, RET_QK_DIM), 1)
        diff = (ri - ci).astype(F32)
        intra[...] = jnp.where(diff >= 0, jnp.exp(lg * jnp.maximum(diff, 0.0)), 0.0)
        rf = ri.astype(F32)
        qd[...] = jnp.exp(lg * (rf + 1.0))
        kd[...] = jnp.exp(lg * (c_len - 1.0 - rf))
        state[...] = jnp.zeros_like(state)

    chunk_decay = qd[c_len - 1:c_len, 0:1]
    for c in range(n_chunks):
        sl = pl.ds(c * c_len, c_len)
        qc = q_ref[0, sl, :]
        kc = k_ref[0, sl, :]
        vc = v_ref[0, sl, :]
        s = _nt_dot(qc, kc) * intra[...]
        st = state[...]
        qdec = (qc.astype(F32) * qd[...]).astype(BF16)
        o = _dot(s.astype(BF16), vc) + _dot(qdec, st.astype(BF16))
        kdec = (kc.astype(F32) * kd[...]).astype(BF16)
        state[...] = st * chunk_decay + lax.dot_general(
            kdec, vc, (((0,), (0,)), ((), ())), preferred_element_type=F32)
        ms = jnp.mean(o * o, axis=-1, keepdims=True)
        y = o * lax.rsqrt(ms + NORM_EPS)
        o_ref[0, sl, :] = (y * g_ref[0, sl, :].astype(F32)).astype(BF16)


def _retention(qk3, v3, g3, tq):
    b, t, _ = qk3.shape
    chunk = RET_CHUNK
    assert RET_CHUNK == RET_QK_DIM and tq % chunk == 0 and t % tq == 0
    kern = functools.partial(_retention_kernel, chunk=chunk, n_chunks=tq // chunk)
    nq = D_MODEL // RET_QK_DIM
    return pl.pallas_call(
        kern,
        grid=(b, RET_HEADS, t // tq),
        in_specs=[pl.BlockSpec((1, tq, RET_QK_DIM), lambda bb, h, tt: (bb, tt, h)),
                  pl.BlockSpec((1, tq, RET_QK_DIM), lambda bb, h, tt: (bb, tt, nq + h)),
                  pl.BlockSpec((1, tq, RET_V_DIM), lambda bb, h, tt: (bb, tt, h)),
                  pl.BlockSpec((1, tq, RET_V_DIM), lambda bb, h, tt: (bb, tt, h))],
        out_specs=pl.BlockSpec((1, tq, RET_V_DIM), lambda bb, h, tt: (bb, tt, h)),
        out_shape=jax.ShapeDtypeStruct((b, t, RET_HEADS * RET_V_DIM), BF16),
        scratch_shapes=[pltpu.VMEM((RET_QK_DIM, RET_V_DIM), F32),
                        pltpu.VMEM((chunk, RET_QK_DIM), F32),
                        pltpu.VMEM((chunk, RET_QK_DIM), F32),
                        pltpu.VMEM((chunk, RET_QK_DIM), F32)],
        compiler_params=pltpu.CompilerParams(dimension_semantics=("arbitrary",) * 3),
        name="retention",
    )(qk3, qk3, v3, g3)


def _matmul_res_kernel(a_ref, w_ref, res_ref, o_ref):
    o_ref[...] = res_ref[...] + _dot(a_ref[...], w_ref[...])


def _matmul_res(a, w_bf16, res, tm):
    m, k = a.shape
    n = w_bf16.shape[1]
    return pl.pallas_call(
        _matmul_res_kernel,
        grid=(m // tm,),
        in_specs=[pl.BlockSpec((tm, k), lambda i: (i, 0)),
                  _resident((k, n), lambda i: (0, 0)),
                  pl.BlockSpec((tm, n), lambda i: (i, 0))],
        out_specs=pl.BlockSpec((tm, n), lambda i: (i, 0)),
        out_shape=jax.ShapeDtypeStruct((m, n), F32),
        compiler_params=pltpu.CompilerParams(dimension_semantics=("arbitrary",)),
        name="matmul_res",
    )(a, w_bf16, res)


def _ffn_kernel(x_ref, gain_ref, wg_ref, wu_ref, wd_ref, *rest, final_norm):
    if final_norm:
        fgain_ref, o_ref = rest
    else:
        (o_ref,) = rest
    x = x_ref[...]
    hn = _rms_to_bf16(x, gain_ref[...])
    a = _dot(hn, wg_ref[...])
    b = _dot(hn, wu_ref[...])
    hid = (a * jax.nn.sigmoid(a) * b).astype(BF16)
    y = x + _dot(hid, wd_ref[...])
    if final_norm:
        ms = jnp.mean(y * y, axis=-1, keepdims=True)
        y = y * lax.rsqrt(ms + NORM_EPS) * fgain_ref[...]
    o_ref[...] = y


def _ffn(x2d, gain, w_gu_bf16, w_down_bf16, tm, final_gain=None):
    m = x2d.shape[0]
    in_specs = [pl.BlockSpec((tm, D_MODEL), lambda i: (i, 0)),
                _resident((1, D_MODEL), lambda i: (0, 0)),
                _resident((D_MODEL, FFN_HIDDEN), lambda i: (0, 0)),
                _resident((D_MODEL, FFN_HIDDEN), lambda i: (0, 1)),
                _resident((FFN_HIDDEN, D_MODEL), lambda i: (0, 0))]
    args = [x2d, gain, w_gu_bf16, w_gu_bf16, w_down_bf16]
    if final_gain is not None:
        in_specs.append(_resident((1, D_MODEL), lambda i: (0, 0)))
        args.append(final_gain)
    return pl.pallas_call(
        functools.partial(_ffn_kernel, final_norm=final_gain is not None),
        grid=(m // tm,),
        in_specs=in_specs,
        out_specs=pl.BlockSpec((tm, D_MODEL), lambda i: (i, 0)),
        out_shape=jax.ShapeDtypeStruct((m, D_MODEL), F32),
        compiler_params=pltpu.CompilerParams(dimension_semantics=("arbitrary",),
                                             vmem_limit_bytes=VMEM_LIMIT),
        name="ffn",
    )(*args)


NSA_QK = NSA_QD + 3 * NSA_KVD
NSA_V = 3 * NSA_KVD
NSA_GATE_PAD = NSA_KV_HEADS * LANES
DEN_ROWS = 16
VAL_ROWS = NSA_HEAD_DIM + DEN_ROWS
Q_SCALE_LOG2 = NSA_HEAD_DIM ** -0.5 * float(np.log2(np.e))


def _partial_rotary(a, c, sa, sb):
    return a * c + pltpu.roll(a, 8, 1) * sa + pltpu.roll(a, LANES - 8, 1) * sb


def _t_rows(x):
    n = x.shape[0] // LANES
    return jnp.concatenate([x[c * LANES:(c + 1) * LANES, :].T for c in range(n)], axis=1)


def _nsa_proj_kernel(x_ref, gain_ref, wqk_ref, wv_ref, wg_ref, c_ref, sa_ref, sb_ref,
                     q_ref, kc_ref, vc_ref, gate_ref, ksa_ref, ksb_ref, kwa_ref, kwb_ref,
                     vse_ref, vso_ref, vwe_ref, vwo_ref, *, t, kt, qt):
    tm = x_ref.shape[0]
    half = LANES // 2
    hn = _rms_to_bf16(x_ref[...], gain_ref[...])
    qk = _dot(hn, wqk_ref[...])
    c = c_ref[...]
    sa = sa_ref[...]
    sb = sb_ref[...]

    def rotated(lo, scale=1.0):
        return _partial_rotary(qk[:, lo:lo + LANES], c, sa, sb) * scale

    for chunk in range(NSA_QD // LANES):
        q_ref[:, chunk * LANES:(chunk + 1) * LANES] = rotated(chunk * LANES, Q_SCALE_LOG2).astype(BF16)
    for chunk in range(NSA_KVD // LANES):
        kc_ref[:, chunk * LANES:(chunk + 1) * LANES] = rotated(NSA_QD + chunk * LANES).astype(BF16)

    lane = lax.broadcasted_iota(jnp.int32, (tm, LANES), 1)
    low = lane < half
    pos = lax.rem(pl.program_id(0) * tm + lax.broadcasted_iota(jnp.int32, (tm, LANES), 0), t)
    onehot = jnp.where((lane & (half - 1)) == lax.shift_right_logical(pos, 6), 1.0, 0.0)
    zero = jnp.zeros((tm, LANES), F32)
    for branch, (a_ref, b_ref, fill) in enumerate(((ksa_ref, ksb_ref, onehot), (kwa_ref, kwb_ref, zero))):
        base = NSA_QD + (1 + branch) * NSA_KVD
        for chunk in range(NSA_KVD // LANES):
            kk = rotated(base + chunk * LANES)
            swapped = pltpu.roll(kk, half, 1)
            a_ref[2 * chunk] = jnp.where(low, kk, fill).astype(BF16)
            b_ref[2 * chunk] = jnp.where(low, fill, swapped).astype(BF16)
            a_ref[2 * chunk + 1] = jnp.where(low, swapped, fill).astype(BF16)
            b_ref[2 * chunk + 1] = jnp.where(low, fill, kk).astype(BF16)

    v = _dot(hn, wv_ref[...])
    vc_ref[...] = v[:, :NSA_KVD].astype(BF16)
    ones = jnp.ones((DEN_ROWS, tm), F32)
    for branch, (e_ref, o_ref, tile) in enumerate(((vse_ref, vso_ref, kt), (vwe_ref, vwo_ref, qt))):
        base = (1 + branch) * NSA_KVD
        for chunk in range(NSA_KVD // LANES):
            v_t = _t_rows(v[:, base + chunk * LANES:base + (chunk + 1) * LANES])
            for par in range(2):
                vg = v_t[par * half:(par + 1) * half]
                even = jnp.concatenate([vg, ones], axis=0).astype(BF16)
                odd = jnp.concatenate([ones, vg], axis=0).astype(BF16)
                for j in range(tm // tile):
                    e_ref[2 * chunk + par, j] = even[:, j * tile:(j + 1) * tile]
                    o_ref[2 * chunk + par, j] = odd[:, j * tile:(j + 1) * tile]
    gate_ref[...] = jax.nn.sigmoid(_dot(hn, wg_ref[...]))


def _nsa_proj(x2d, gain, wqk, wv, wg, c, sa, sb, tm, t, kt, qt):
    m = x2d.shape[0]
    g = NSA_KV_HEADS
    assert SLC_BLOCK == 64 and t % tm == 0 and tm % kt == 0 and tm % qt == 0
    row = lambda i: (i, 0)
    fixed = lambda i: (0, 0)
    key_spec = pl.BlockSpec((g, tm, LANES), lambda i: (0, i, 0))
    key_shape = jax.ShapeDtypeStruct((g, m, LANES), BF16)

    def val(tile):
        return (pl.BlockSpec((g, tm // tile, VAL_ROWS, tile), lambda i: (0, i, 0, 0)),
                jax.ShapeDtypeStruct((g, m // tile, VAL_ROWS, tile), BF16))

    return pl.pallas_call(
        functools.partial(_nsa_proj_kernel, t=t, kt=kt, qt=qt),
        grid=(m // tm,),
        in_specs=[pl.BlockSpec((tm, D_MODEL), row),
                  _resident((1, D_MODEL), fixed),
                  _resident((D_MODEL, NSA_QK), fixed),
                  _resident((D_MODEL, NSA_V), fixed),
                  _resident((D_MODEL, NSA_GATE_PAD), fixed),
                  pl.BlockSpec((tm, LANES), row),
                  pl.BlockSpec((tm, LANES), row),
                  pl.BlockSpec((tm, LANES), row)],
        out_specs=[pl.BlockSpec((tm, NSA_QD), row),
                   pl.BlockSpec((tm, NSA_KVD), row),
                   pl.BlockSpec((tm, NSA_KVD), row),
                   pl.BlockSpec((tm, NSA_GATE_PAD), row),
                   key_spec, key_spec, key_spec, key_spec,
                   val(kt)[0], val(kt)[0], val(qt)[0], val(qt)[0]],
        out_shape=[jax.ShapeDtypeStruct((m, NSA_QD), BF16),
                   jax.ShapeDtypeStruct((m, NSA_KVD), BF16),
                   jax.ShapeDtypeStruct((m, NSA_KVD), BF16),
                   jax.ShapeDtypeStruct((m, NSA_GATE_PAD), F32),
                   key_shape, key_shape, key_shape, key_shape,
                   val(kt)[1], val(kt)[1], val(qt)[1], val(qt)[1]],
        compiler_params=pltpu.CompilerParams(dimension_semantics=("arbitrary",),
                                             vmem_limit_bytes=VMEM_LIMIT),
        name="nsa_proj",
    )(x2d, gain, wqk, wv, wg, c, sa, sb)


def _compress_kernel(a_ref, pos_ref, w1_ref, w2_ref, o_ref):
    a = a_ref[0, 0, 0]
    nc = a.shape[0]
    half = CMP_STRIDE * NSA_HEAD_DIM
    upper = _dot(a, w1_ref[0, :half, :])
    lower = _dot(a, w1_ref[0, half:, :])
    cpos = _dot(pos_ref[0], w1_ref[0])
    hid = upper + pltpu.roll(lower, nc - 1, 0) + cpos[0:1, :]
    hid = hid * jax.nn.sigmoid(hid)
    o_ref[0, 0, 0] = _dot(hid.astype(BF16), w2_ref[0]).astype(BF16)


def _compress(a2, pos8, w1, w2):
    _, b, g, nc, width = a2.shape
    return pl.pallas_call(
        _compress_kernel,
        grid=(2, b, g),
        in_specs=[pl.BlockSpec((1, 1, 1, nc, width), lambda s, bb, gg: (s, bb, gg, 0, 0)),
                  pl.BlockSpec((1, 8, width * 2), lambda s, bb, gg: (s, 0, 0)),
                  pl.BlockSpec((1, width * 2, CMP_HIDDEN), lambda s, bb, gg: (s, 0, 0)),
                  pl.BlockSpec((1, CMP_HIDDEN, NSA_HEAD_DIM), lambda s, bb, gg: (s, 0, 0))],
        out_specs=pl.BlockSpec((1, 1, 1, nc, NSA_HEAD_DIM), lambda s, bb, gg: (s, bb, gg, 0, 0)),
        out_shape=jax.ShapeDtypeStruct((2, b, g, nc, NSA_HEAD_DIM), BF16),
        compiler_params=pltpu.CompilerParams(dimension_semantics=("arbitrary",) * 3),
        name="nsa_compress",
    )(a2, pos8, w1, w2)


def _t_cols(x):
    n = x.shape[1] // LANES
    return jnp.concatenate([x[:, c * LANES:(c + 1) * LANES].T for c in range(n)], axis=0)


def _flash_update(par, st, v_aug, m_sc, acc_sc):
    m_old = m_sc[par]
    m_new = jnp.maximum(m_old, jnp.max(st, axis=0, keepdims=True))
    p = jnp.exp2(st - m_new)
    acc_sc[par] = acc_sc[par] * jnp.exp2(m_old - m_new) + _dot(v_aug, p.astype(BF16))
    m_sc[par] = m_new


def _normalised(acc, par):
    if par == 0:
        return acc[:NSA_HEAD_DIM] * (1.0 / acc[NSA_HEAD_DIM:NSA_HEAD_DIM + 1])
    return acc[DEN_ROWS:] * (1.0 / acc[0:1])


def _nsa_attn_kernel(q_ref, gate_ref, ovl_ref, kca_ref, kcb_ref, vc_ref,
                     ksa_ref, ksb_ref, vse_ref, vso_ref, kwa_ref, kwb_ref, vwe_ref, vwo_ref,
                     *rest, qi, qt, kt, n_sel, n_back):
    o_ref, qm_sc, qaug_sc, x_sc, tot_sc, m_sc, acc_sc, sta_sc, stb_sc = rest[-9:]
    t0 = qi * qt
    half = LANES // 2
    wide = 2 * qt
    lane = lax.broadcasted_iota(jnp.int32, (qt, LANES), 1)
    low = lane < NSA_HEAD_DIM
    zero = jnp.zeros((qt, LANES), BF16)
    qps = [q_ref[0, :, pair * LANES:(pair + 1) * LANES] for pair in range(2)]
    for pair in range(2):
        rows = slice(pair * qt, (pair + 1) * qt)
        qm_sc[0, rows] = jnp.where(low, qps[pair], zero)
        qm_sc[1, rows] = jnp.where(low, zero, qps[pair])

    gates_t = _t_rows(gate_ref[0])

    def gate_row(branch, par):
        return jnp.concatenate([gates_t[branch * 4 + 2 * pair + par:branch * 4 + 2 * pair + par + 1, :]
                                for pair in range(2)], axis=1)

    col = lax.broadcasted_iota(jnp.int32, (1, wide), 1)
    tcol = t0 + jnp.where(col >= qt, col - qt, col)

    t_last = t0 + qt - 1
    nc = min(kca_ref.shape[2], -(-((t_last - CMP_BLOCK + 1) // CMP_STRIDE + 1) // LANES) * LANES)
    kidx = lax.broadcasted_iota(jnp.int32, (nc, wide), 0)
    valid = kidx * CMP_STRIDE + (CMP_BLOCK - 1) <= tcol
    vct = vc_ref[0, 0, :, :nc]
    has_block = tcol >= CMP_BLOCK - 1
    psum = None
    for par in range(2):
        st = _nt_dot((kcb_ref if par else kca_ref)[0, 0, :nc, :], qm_sc[par])
        st = jnp.where(valid, st, NEG)
        m = jnp.max(st, axis=0, keepdims=True)
        e = jnp.exp2(st - m)
        l = jnp.sum(e, axis=0, keepdims=True)
        p = e * jnp.where(has_block, 1.0 / l, 0.0)
        ph = p[:, :qt] + p[:, qt:]
        psum = ph if psum is None else psum + ph
        tot_sc[par] = _dot(vct, p.astype(BF16)) * gate_row(0, par)

    ovl = ovl_ref[:, :nc]
    hi = psum.astype(BF16)
    r1 = psum - hi.astype(F32)
    mid = r1.astype(BF16)
    lo = (r1 - mid.astype(F32)).astype(BF16)
    imp = (_dot(ovl, hi) + _dot(ovl, mid) + _dot(ovl, lo))[:half, :]
    nio = lax.broadcasted_iota(jnp.int32, (half, qt), 0)
    cur = lax.shift_right_logical(t0 + lax.broadcasted_iota(jnp.int32, (half, qt), 1), 6)
    forced = (nio == 0) | (nio == cur) | (nio == cur - 1)
    x_sc[...] = jnp.where(nio > cur, NEG, jnp.where(forced, imp + FORCE, imp))

    sub = lax.broadcasted_iota(jnp.int32, (SUBLANES, qt), 0)
    n_live = (t_last // SLC_BLOCK) // SUBLANES + 1
    xs = [x_sc[r * SUBLANES:(r + 1) * SUBLANES, :] for r in range(n_live)]
    cnts = [jnp.zeros((SUBLANES, qt), jnp.int32) for _ in range(n_live)]
    for rm in range(n_live):
        for mm in range(SUBLANES):
            row = xs[rm][mm:mm + 1, :]
            for r in range(n_live):
                if r < rm:
                    beats = row > xs[r]
                elif r > rm:
                    beats = row >= xs[r]
                else:
                    beats = (row > xs[r]) | ((row == xs[r]) & (sub > mm))
                cnts[r] = cnts[r] + jnp.where(beats, 1, 0)
    bias_t = jnp.where(jnp.concatenate(cnts, axis=0) < n_sel, 0.0, NEG).astype(F32)
    if n_live * SUBLANES < half:
        bias_t = jnp.concatenate([bias_t, jnp.full((half - n_live * SUBLANES, qt), NEG, F32)], axis=0)
    bias = _t_cols(jnp.concatenate([bias_t, bias_t], axis=0)).astype(BF16)
    for pair in range(2):
        rows = slice(pair * qt, (pair + 1) * qt)
        qaug_sc[0, rows] = jnp.where(low, qps[pair], bias)
        qaug_sc[1, rows] = jnp.where(low, bias, qps[pair])

    ri = lax.broadcasted_iota(jnp.int32, (qt, wide), 0)
    ci = lax.broadcasted_iota(jnp.int32, (qt, wide), 1)
    ci = jnp.where(ci >= qt, ci - qt, ci)
    backs = [back for back in range(n_back + 1) if qi >= back]
    for par in range(2):
        k_ref = kwb_ref if par else kwa_ref
        v_ref = vwo_ref if par else vwe_ref
        sts = []
        for back in backs:
            st = _nt_dot(k_ref[0, (qi - back) * qt:(qi - back + 1) * qt, :], qm_sc[par])
            if back == 0:
                st = jnp.where(ri <= ci, st, NEG)
            elif back == n_back:
                st = jnp.where(ri > ci, st, NEG)
            sts.append(st)
        m = functools.reduce(jnp.maximum, [jnp.max(st, axis=0, keepdims=True) for st in sts])
        acc = None
        for back, st in zip(backs, sts):
            part = _dot(v_ref[0, qi - back], jnp.exp2(st - m).astype(BF16))
            acc = part if acc is None else acc + part
        tot_sc[par] = tot_sc[par] + _normalised(acc, par) * gate_row(2, par)

    m_sc[...] = jnp.full(m_sc.shape, NEG, F32)
    acc_sc[...] = jnp.zeros_like(acc_sc)

    def scores(kidx, buf):
        ks = pl.multiple_of(kidx * kt, kt)
        buf[0] = _nt_dot(ksa_ref[0, pl.ds(ks, kt), :], qaug_sc[0])
        buf[1] = _nt_dot(ksb_ref[0, pl.ds(ks, kt), :], qaug_sc[1])

    def consume(kidx, buf, masked):
        v_aug = (vse_ref[0, kidx], vso_ref[0, kidx])
        if masked:
            keep = kidx * kt + lax.broadcasted_iota(jnp.int32, (kt, wide), 0) <= tcol
        sts, m_olds, m_news = [], [], []
        for par in range(2):
            st = buf[par]
            if masked:
                st = jnp.where(keep, st, NEG)
            sts.append(st)
            m_olds.append(m_sc[par])
            m_news.append(jnp.maximum(m_olds[par], jnp.max(st, axis=0, keepdims=True)))
        ps = [jnp.exp2(sts[par] - m_news[par]).astype(BF16) for par in range(2)]
        for par in range(2):
            acc_sc[par] = acc_sc[par] * jnp.exp2(m_olds[par] - m_news[par]) + _dot(v_aug[par], ps[par])
            m_sc[par] = m_news[par]

    assert qt <= kt
    last = t_last // kt
    scores(0, sta_sc)

    def body(j, carry):
        scores(2 * j + 1, stb_sc)
        consume(2 * j, sta_sc, False)
        scores(2 * j + 2, sta_sc)
        consume(2 * j + 1, stb_sc, False)
        return carry

    if last >= 2:
        lax.fori_loop(0, last // 2, body, 0)
    if last % 2 == 0:
        consume(last, sta_sc, True)
    else:
        scores(last, stb_sc)
        consume(last - 1, sta_sc, False)
        consume(last, stb_sc, True)

    total = [tot_sc[par] + _normalised(acc_sc[par], par) * gate_row(1, par) for par in range(2)]
    for pair in range(2):
        cols = slice(pair * qt, (pair + 1) * qt)
        pair_t = jnp.concatenate([total[0][:, cols], total[1][:, cols]], axis=0)
        o_ref[0, :, pair * LANES:(pair + 1) * LANES] = _t_cols(pair_t).astype(BF16)


def _nsa_attn(qk3, gates3, ovl, kca, kcb, vc2, ksa, ksb, vse, vso, kwa, kwb, vwe, vwo, qt, kt):
    b, t, _ = qk3.shape
    g = NSA_KV_HEADS
    nc = kca.shape[2]
    n_blocks = t // SLC_BLOCK
    assert n_blocks <= LANES // 2 and SLC_BLOCK == 64 and WINDOW % qt == 0
    per_group = lambda *tail: (lambda bb, gg: (bb, gg) + tail)
    kc_spec = pl.BlockSpec((1, 1, nc, LANES), per_group(0, 0))
    k_spec = pl.BlockSpec((1, t, LANES), lambda bb, gg: (gg, bb, 0))
    vs_spec = pl.BlockSpec((1, t // kt, VAL_ROWS, kt), lambda bb, gg: (gg, bb, 0, 0))
    vw_spec = pl.BlockSpec((1, t // qt, VAL_ROWS, qt), lambda bb, gg: (gg, bb, 0, 0))
    half = LANES // 2
    args = [qk3, gates3, ovl, kca, kcb, vc2, ksa, ksb, vse, vso, kwa, kwb, vwe, vwo]
    out = None
    for qi in range(t // qt):
        tile_map = lambda bb, gg, qi=qi: (bb, qi, gg)
        in_specs = [pl.BlockSpec((1, qt, MXU_DIM), tile_map),
                    pl.BlockSpec((1, qt, LANES), tile_map),
                    pl.BlockSpec((LANES, nc), lambda bb, gg: (0, 0)),
                    kc_spec, kc_spec,
                    pl.BlockSpec((1, 1, half, nc), per_group(0, 0)),
                    k_spec, k_spec, vs_spec, vs_spec,
                    k_spec, k_spec, vw_spec, vw_spec]
        if out is not None:
            in_specs.append(pl.BlockSpec(memory_space=pl.ANY))
        out = pl.pallas_call(
            functools.partial(_nsa_attn_kernel, qi=qi, qt=qt, kt=kt,
                              n_sel=min(SLC_TOPK, n_blocks), n_back=WINDOW // qt),
            grid=(b, g),
            in_specs=in_specs,
            out_specs=pl.BlockSpec((1, qt, MXU_DIM), tile_map),
            out_shape=jax.ShapeDtypeStruct((b, t, NSA_QD), BF16),
            scratch_shapes=[pltpu.VMEM((2, 2 * qt, LANES), BF16),
                            pltpu.VMEM((2, 2 * qt, LANES), BF16),
                            pltpu.VMEM((half, qt), F32),
                            pltpu.VMEM((2, half, 2 * qt), F32),
                            pltpu.VMEM((2, 1, 2 * qt), F32),
                            pltpu.VMEM((2, VAL_ROWS, 2 * qt), F32),
                            pltpu.VMEM((2, kt, 2 * qt), F32),
                            pltpu.VMEM((2, kt, 2 * qt), F32)],
            input_output_aliases={} if out is None else {len(args): 0},
            compiler_params=pltpu.CompilerParams(dimension_semantics=("arbitrary",) * 2,
                                                 vmem_limit_bytes=VMEM_LIMIT),
            name=f"nsa_attn_q{qi}",
        )(*(args if out is None else args + [out]))
    return out


def _ret_rope_consts():
    half = RET_QK_DIM // 2
    inv_freq = RET_ROPE_THETA ** (-2.0 * jnp.arange(half, dtype=F32) / RET_QK_DIM)
    rows = jnp.zeros((8, LANES), F32)
    return rows.at[0].set(inv_freq).at[1].set(1.0)


def _nsa_rope_consts():
    half = NSA_ROPE_DIM // 2
    inv_freq = NSA_ROPE_THETA ** (-2.0 * jnp.arange(half, dtype=F32) / NSA_ROPE_DIM)
    r = np.arange(LANES) % NSA_HEAD_DIM
    rot = r < NSA_ROPE_DIM
    freq = jnp.where(rot, inv_freq[r % half], 0.0)
    mask_a = ((r >= half) & rot).astype(np.float32)
    mask_b = -(r < half).astype(np.float32)
    rows = jnp.zeros((8, LANES), F32)
    return rows.at[0].set(freq).at[1].set(mask_a).at[2].set(mask_b)


def _nsa_weight_columns():
    kv0 = NSA_QD
    cols = list(range(NSA_QD))
    for kvsel in range(2):
        for branch in range(3):
            base = kv0 + (branch * 2 + kvsel) * NSA_KVD
            cols += list(range(base, base + NSA_KVD))
    gate0 = NSA_QD + 6 * NSA_KVD
    hg = NSA_HEADS // NSA_KV_HEADS
    for g in range(NSA_KV_HEADS):
        lanes = [-1] * LANES
        for branch in range(3):
            for hh in range(hg):
                lanes[branch * 4 + hh] = gate0 + (g * hg + hh) * 3 + branch
        cols += lanes
    return np.asarray(cols, np.int32)


def _group_major(a, b, t):
    return a.reshape(b, t, NSA_KV_HEADS, NSA_HEAD_DIM).transpose(0, 2, 1, 3)


def _pad_pair(k):
    z = jnp.zeros_like(k)
    return jnp.concatenate([k, z], axis=-1), jnp.concatenate([z, k], axis=-1)


def _retention_layer(x2d, b, t, gain, w_in, w_out, cos, sin, tm):
    qk, v, gate = _ret_proj(x2d, gain.reshape(1, D_MODEL), w_in.astype(BF16), cos, sin, tm)
    wide = lambda a: a.reshape(b, t, 2 * D_MODEL)
    o = _retention(wide(qk), wide(v), wide(gate), min(t, 1024))
    return _matmul_res(o.reshape(b * t, RET_HEADS * RET_V_DIM), w_out.astype(BF16), x2d, tm)


def _nsa_layer(x2d, b, t, gain, w_in, cmp_pos, cmp_w1, cmp_w2, w_out, tabs, tm):
    g = NSA_KV_HEADS
    cols = _nsa_weight_columns()
    pick = lambda lo, hi: jnp.concatenate(
        [w_in[:, c0:c0 + NSA_KVD] for c0 in cols[lo:hi:NSA_KVD]], axis=1).astype(BF16)
    gate_cols = cols[NSA_QK + NSA_V:]
    w_ext = jnp.concatenate([w_in, jnp.zeros((D_MODEL, 1), w_in.dtype)], axis=1)
    w_gate = jnp.take(w_ext, jnp.asarray(np.where(gate_cols < 0, w_in.shape[1], gate_cols)),
                      axis=1).astype(BF16)
    qt = min(t, 512)
    kt = min(t, 512)
    q, k_cmp_tok, v_cmp_tok, gates, *slc_win = _nsa_proj(
        x2d, gain.reshape(1, D_MODEL), pick(0, NSA_QK), pick(NSA_QK, NSA_QK + NSA_V), w_gate,
        *tabs, tm, t, kt, qt)
    ksa, ksb, kwa, kwb, vse, vso, vwe, vwo = slc_win

    nc = t // CMP_STRIDE
    a2 = jnp.stack([_group_major(k_cmp_tok, b, t), _group_major(v_cmp_tok, b, t)])
    a2 = a2.reshape(2, b, g, nc, CMP_STRIDE * NSA_HEAD_DIM)
    pos8 = jnp.broadcast_to(cmp_pos.reshape(2, 1, CMP_BLOCK * NSA_HEAD_DIM),
                            (2, 8, CMP_BLOCK * NSA_HEAD_DIM)).astype(BF16)
    cmp = _compress(a2, pos8, cmp_w1.astype(BF16), cmp_w2.astype(BF16))
    kca, kcb = _pad_pair(cmp[0])
    vc2 = cmp[1].transpose(0, 1, 3, 2)

    n_blocks = t // SLC_BLOCK
    kk = np.arange(nc)[None, :]
    nn = np.arange(LANES)[:, None]
    ovl = ((kk * CMP_STRIDE < nn * SLC_BLOCK + SLC_BLOCK) & (kk * CMP_STRIDE + CMP_BLOCK > nn * SLC_BLOCK)
           & (nn < n_blocks) & (kk < nc - 1))
    ovl = jnp.asarray(ovl, BF16)

    attn = _nsa_attn(q.reshape(b, t, NSA_QD), gates.reshape(b, t, NSA_GATE_PAD), ovl, kca, kcb, vc2,
                     ksa, ksb, vse, vso, kwa, kwb, vwe, vwo, qt, kt)
    return _matmul_res(attn.reshape(b * t, NSA_QD), w_out.astype(BF16), x2d, tm)


def _ffn_layer(x2d, gain, w_gu, w_down, tm, final_gain=None):
    fg = None if final_gain is None else final_gain.reshape(1, D_MODEL)
    return _ffn(x2d, gain.reshape(1, D_MODEL), w_gu.astype(BF16), w_down.astype(BF16), tm, final_gain=fg)


def kernel(x, positions, norm_mix, norm_ffn, norm_final, ret_w_in, ret_w_out, nsa_w_in, nsa_cmp_pos,
           nsa_cmp_w1, nsa_cmp_w2, nsa_w_out, ffn_w_gu, ffn_w_down):
    b, t, _ = x.shape
    m = b * t
    tm = min(m, ROW_TILE)
    x2d = x.reshape(m, D_MODEL)
    pos_col = positions.reshape(m, 1).astype(jnp.int32)
    ret_cos, ret_sin, _ = _rope_tables(pos_col, _ret_rope_consts(), tm)
    nsa_tabs = _rope_tables(pos_col, _nsa_rope_consts(), tm)
    for i in range(DEPTH):
        j = i // 2
        if i % 2 == 0:
            x2d = _retention_layer(x2d, b, t, norm_mix[i], ret_w_in[j], ret_w_out[j], ret_cos, ret_sin, tm)
        else:
            x2d = _nsa_layer(x2d, b, t, norm_mix[i], nsa_w_in[j], nsa_cmp_pos[j], nsa_cmp_w1[j],
                             nsa_cmp_w2[j], nsa_w_out[j], nsa_tabs, tm)
        x2d = _ffn_layer(x2d, norm_ffn[i], ffn_w_gu[i], ffn_w_down[i], tm,
                         final_gain=norm_final if i == DEPTH - 1 else None)
    return x2d.reshape(b, t, D_MODEL)
```

```python
import functools

import numpy as np
import jax
import jax.numpy as jnp
from jax import lax
from jax.experimental import pallas as pl
from jax.experimental.pallas import tpu as pltpu

F32 = jnp.float32
BF16 = jnp.bfloat16

D_MODEL = 1024
DEPTH = 4
NORM_EPS = 1e-6
NEG = -1e30

RET_HEADS = 4
RET_QK_DIM = 256
RET_V_DIM = 512
RET_ROPE_THETA = 10000.0
RET_CHUNK = 256

NSA_HEADS = 16
NSA_KV_HEADS = 4
NSA_HEAD_DIM = 64
NSA_ROPE_THETA = 500000.0
NSA_ROPE_DIM = 16
CMP_BLOCK = 32
CMP_STRIDE = 16
CMP_HIDDEN = 256
SLC_BLOCK = 64
SLC_TOPK = 16
WINDOW = 512
FORCE = 1e4
NSA_QD = 1024
NSA_KVD = 256
FFN_HIDDEN = 2816

LANES = 128
SUBLANES = 8
MXU_DIM = 256
ROW_TILE = 512
VMEM_LIMIT = 56 * 1024 * 1024


def _nt_dot(a, b):
    return lax.dot_general(a, b, (((1,), (1,)), ((), ())), preferred_element_type=F32)


def _dot(a, b):
    return jnp.dot(a, b, preferred_element_type=F32)


def _rms_to_bf16(x, gain):
    ms = jnp.mean(x * x, axis=-1, keepdims=True)
    return (x * lax.rsqrt(ms + NORM_EPS) * gain).astype(BF16)


def _resident(shape, index_map):
    return pl.BlockSpec(shape, index_map, pipeline_mode=pl.Buffered(1))


def _rope_tables_kernel(pos_ref, c_ref, cos_ref, sa_ref, sb_ref):
    ang = pos_ref[...].astype(F32) * c_ref[0:1, :]
    s = jnp.sin(ang)
    cos_ref[...] = jnp.cos(ang)
    sa_ref[...] = s * c_ref[1:2, :]
    sb_ref[...] = s * c_ref[2:3, :]


def _rope_tables(pos_col, consts, tm):
    m = pos_col.shape[0]
    out = jax.ShapeDtypeStruct((m, LANES), F32)
    return pl.pallas_call(
        _rope_tables_kernel,
        grid=(m // tm,),
        in_specs=[pl.BlockSpec((tm, 1), lambda i: (i, 0)),
                  pl.BlockSpec((8, LANES), lambda i: (0, 0))],
        out_specs=[pl.BlockSpec((tm, LANES), lambda i: (i, 0))] * 3,
        out_shape=[out, out, out],
        name="rope_tables",
    )(pos_col, consts)


def _ret_proj_kernel(x_ref, gain_ref, wqk_ref, wv_ref, wg_ref, cos_ref, sin_ref, qk_ref, v_ref, g_ref):
    hn = _rms_to_bf16(x_ref[...], gain_ref[...])
    qk = _dot(hn, wqk_ref[...])
    c = cos_ref[...]
    s = sin_ref[...]
    for head in range(2 * RET_HEADS):
        scale = 1.0 if head < RET_HEADS else RET_QK_DIM ** -0.5
        lo = head * RET_QK_DIM
        x1 = qk[:, lo:lo + LANES]
        x2 = qk[:, lo + LANES:lo + 2 * LANES]
        qk_ref[:, lo:lo + LANES] = ((x1 * c - x2 * s) * scale).astype(BF16)
        qk_ref[:, lo + LANES:lo + 2 * LANES] = ((x2 * c + x1 * s) * scale).astype(BF16)
    v_ref[...] = _dot(hn, wv_ref[...]).astype(BF16)
    gate = _dot(hn, wg_ref[...])
    g_ref[...] = (gate * jax.nn.sigmoid(gate)).astype(BF16)


def _ret_proj(x2d, gain, w_bf16, cos, sin, tm):
    m = x2d.shape[0]
    width = 2 * D_MODEL
    out = jax.ShapeDtypeStruct((m, width), BF16)
    return pl.pallas_call(
        _ret_proj_kernel,
        grid=(m // tm,),
        in_specs=[pl.BlockSpec((tm, D_MODEL), lambda i: (i, 0)),
                  _resident((1, D_MODEL), lambda i: (0, 0)),
                  _resident((D_MODEL, width), lambda i: (0, 0)),
                  _resident((D_MODEL, width), lambda i: (0, 1)),
                  _resident((D_MODEL, width), lambda i: (0, 2)),
                  pl.BlockSpec((tm, LANES), lambda i: (i, 0)),
                  pl.BlockSpec((tm, LANES), lambda i: (i, 0))],
        out_specs=[pl.BlockSpec((tm, width), lambda i: (i, 0))] * 3,
        out_shape=[out, out, out],
        compiler_params=pltpu.CompilerParams(dimension_semantics=("arbitrary",),
                                             vmem_limit_bytes=VMEM_LIMIT),
        name="ret_proj",
    )(x2d, gain, w_bf16, w_bf16, w_bf16, cos, sin)


def _retention_kernel(q_ref, k_ref, v_ref, g_ref, o_ref, state, intra, qd, kd, *, chunk, n_chunks):
    c_len = chunk
    h = pl.program_id(1)
    t = pl.program_id(2)

    @pl.when(t == 0)
    def _():
        hv = jnp.full((c_len, RET_QK_DIM), h, jnp.int32)
        den = jnp.left_shift(jnp.full((c_len, RET_QK_DIM), 32, jnp.int32), hv).astype(F32)
        lg = jnp.log(1.0 - 1.0 / den)
        ri = lax.broadcasted_iota(jnp.int32, (c_len, RET_QK_DIM), 0)
        ci = lax.broadcasted_iota(jnp.int32, (c_len, RET_QK_DIM), 1)
        diff = (ri - ci).astype(F32)
        intra[...] = jnp.where(diff >= 0, jnp.exp(lg * jnp.maximum(diff, 0.0)), 0.0)
        rf = ri.astype(F32)
        qd[...] = jnp.exp(lg * (rf + 1.0))
        kd[...] = jnp.exp(lg * (c_len - 1.0 - rf))
        state[...] = jnp.zeros_like(state)

    chunk_decay = qd[c_len - 1:c_len, 0:1]
    for c in range(n_chunks):
        sl = pl.ds(c * c_len, c_len)
        qc = q_ref[0, sl, :]
        kc = k_ref[0, sl, :]
        vc = v_ref[0, sl, :]
        s = _nt_dot(qc, kc) * intra[...]
        st = state[...]
        qdec = (qc.astype(F32) * qd[...]).astype(BF16)
        o = _dot(s.astype(BF16), vc) + _dot(qdec, st.astype(BF16))
        kdec = (kc.astype(F32) * kd[...]).astype(BF16)
        state[...] = st * chunk_decay + lax.dot_general(
            kdec, vc, (((0,), (0,)), ((), ())), preferred_element_type=F32)
        ms = jnp.mean(o * o, axis=-1, keepdims=True)
        y = o * lax.rsqrt(ms + NORM_EPS)
        o_ref[0, sl, :] = (y * g_ref[0, sl, :].astype(F32)).astype(BF16)


def _retention(qk3, v3, g3, tq):
    b, t, _ = qk3.shape
    chunk = RET_CHUNK
    assert RET_CHUNK == RET_QK_DIM and tq % chunk == 0 and t % tq == 0
    kern = functools.partial(_retention_kernel, chunk=chunk, n_chunks=tq // chunk)
    nq = D_MODEL // RET_QK_DIM
    return pl.pallas_call(
        kern,
        grid=(b, RET_HEADS, t // tq),
        in_specs=[pl.BlockSpec((1, tq, RET_QK_DIM), lambda bb, h, tt: (bb, tt, h)),
                  pl.BlockSpec((1, tq, RET_QK_DIM), lambda bb, h, tt: (bb, tt, nq + h)),
                  pl.BlockSpec((1, tq, RET_V_DIM), lambda bb, h, tt: (bb, tt, h)),
                  pl.BlockSpec((1, tq, RET_V_DIM), lambda bb, h, tt: (bb, tt, h))],
        out_specs=pl.BlockSpec((1, tq, RET_V_DIM), lambda bb, h, tt: (bb, tt, h)),
        out_shape=jax.ShapeDtypeStruct((b, t, RET_HEADS * RET_V_DIM), BF16),
        scratch_shapes=[pltpu.VMEM((RET_QK_DIM, RET_V_DIM), F32),
                        pltpu.VMEM((chunk, RET_QK_DIM), F32),
                        pltpu.VMEM((chunk, RET_QK_DIM), F32),
                        pltpu.VMEM((chunk, RET_QK_DIM), F32)],
        compiler_params=pltpu.CompilerParams(dimension_semantics=("arbitrary",) * 3),
        name="retention",
    )(qk3, qk3, v3, g3)


def _matmul_res_kernel(a_ref, w_ref, res_ref, o_ref):
    o_ref[...] = res_ref[...] + _dot(a_ref[...], w_ref[...])


def _matmul_res(a, w_bf16, res, tm):
    m, k = a.shape
    n = w_bf16.shape[1]
    return pl.pallas_call(
        _matmul_res_kernel,
        grid=(m // tm,),
        in_specs=[pl.BlockSpec((tm, k), lambda i: (i, 0)),
                  _resident((k, n), lambda i: (0, 0)),
                  pl.BlockSpec((tm, n), lambda i: (i, 0))],
        out_specs=pl.BlockSpec((tm, n), lambda i: (i, 0)),
        out_shape=jax.ShapeDtypeStruct((m, n), F32),
        compiler_params=pltpu.CompilerParams(dimension_semantics=("arbitrary",)),
        name="matmul_res",
    )(a, w_bf16, res)


def _ffn_kernel(x_ref, gain_ref, wg_ref, wu_ref, wd_ref, *rest, final_norm):
    if final_norm:
        fgain_ref, o_ref = rest
    else:
        (o_ref,) = rest
    x = x_ref[...]
    hn = _rms_to_bf16(x, gain_ref[...])
    a = _dot(hn, wg_ref[...])
    b = _dot(hn, wu_ref[...])
    hid = (a * jax.nn.sigmoid(a) * b).astype(BF16)
    y = x + _dot(hid, wd_ref[...])
    if final_norm:
        ms = jnp.mean(y * y, axis=-1, keepdims=True)
        y = y * lax.rsqrt(ms + NORM_EPS) * fgain_ref[...]
    o_ref[...] = y


def _ffn(x2d, gain, w_gu_bf16, w_down_bf16, tm, final_gain=None):
    m = x2d.shape[0]
    in_specs = [pl.BlockSpec((tm, D_MODEL), lambda i: (i, 0)),
                _resident((1, D_MODEL), lambda i: (0, 0)),
                _resident((D_MODEL, FFN_HIDDEN), lambda i: (0, 0)),
                _resident((D_MODEL, FFN_HIDDEN), lambda i: (0, 1)),
                _resident((FFN_HIDDEN, D_MODEL), lambda i: (0, 0))]
    args = [x2d, gain, w_gu_bf16, w_gu_bf16, w_down_bf16]
    if final_gain is not None:
        in_specs.append(_resident((1, D_MODEL), lambda i: (0, 0)))
        args.append(final_gain)
    return pl.pallas_call(
        functools.partial(_ffn_kernel, final_norm=final_gain is not None),
        grid=(m // tm,),
        in_specs=in_specs,
        out_specs=pl.BlockSpec((tm, D_MODEL), lambda i: (i, 0)),
        out_shape=jax.ShapeDtypeStruct((m, D_MODEL), F32),
        compiler_params=pltpu.CompilerParams(dimension_semantics=("arbitrary",),
                                             vmem_limit_bytes=VMEM_LIMIT),
        name="ffn",
    )(*args)


NSA_QK = NSA_QD + 3 * NSA_KVD
NSA_V = 3 * NSA_KVD
NSA_GATE_PAD = NSA_KV_HEADS * LANES
DEN_ROWS = 16
VAL_ROWS = NSA_HEAD_DIM + DEN_ROWS
Q_SCALE_LOG2 = NSA_HEAD_DIM ** -0.5 * float(np.log2(np.e))


def _partial_rotary(a, c, sa, sb):
    return a * c + pltpu.roll(a, 8, 1) * sa + pltpu.roll(a, LANES - 8, 1) * sb


def _t_rows(x):
    n = x.shape[0] // LANES
    return jnp.concatenate([x[c * LANES:(c + 1) * LANES, :].T for c in range(n)], axis=1)


def _nsa_proj_kernel(x_ref, gain_ref, wqk_ref, wv_ref, wg_ref, c_ref, sa_ref, sb_ref,
                     q_ref, kc_ref, vc_ref, gate_ref, ksa_ref, ksb_ref, kwa_ref, kwb_ref,
                     vse_ref, vso_ref, vwe_ref, vwo_ref, *, t, kt, qt):
    tm = x_ref.shape[0]
    half = LANES // 2
    hn = _rms_to_bf16(x_ref[...], gain_ref[...])
    qk = _dot(hn, wqk_ref[...])
    c = c_ref[...]
    sa = sa_ref[...]
    sb = sb_ref[...]

    def rotated(lo, scale=1.0):
        return _partial_rotary(qk[:, lo:lo + LANES], c, sa, sb) * scale

    for chunk in range(NSA_QD // LANES):
        q_ref[:, chunk * LANES:(chunk + 1) * LANES] = rotated(chunk * LANES, Q_SCALE_LOG2).astype(BF16)
    for chunk in range(NSA_KVD // LANES):
        kc_ref[:, chunk * LANES:(chunk + 1) * LANES] = rotated(NSA_QD + chunk * LANES).astype(BF16)

    lane = lax.broadcasted_iota(jnp.int32, (tm, LANES), 1)
    low = lane < half
    pos = lax.rem(pl.program_id(0) * tm + lax.broadcasted_iota(jnp.int32, (tm, LANES), 0), t)
    onehot = jnp.where((lane & (half - 1)) == lax.shift_right_logical(pos, 6), 1.0, 0.0)
    zero = jnp.zeros((tm, LANES), F32)
    for branch, (a_ref, b_ref, fill) in enumerate(((ksa_ref, ksb_ref, onehot), (kwa_ref, kwb_ref, zero))):
        base = NSA_QD + (1 + branch) * NSA_KVD
        for chunk in range(NSA_KVD // LANES):
            kk = rotated(base + chunk * LANES)
            swapped = pltpu.roll(kk, half, 1)
            a_ref[2 * chunk] = jnp.where(low, kk, fill).astype(BF16)
            b_ref[2 * chunk] = jnp.where(low, fill, swapped).astype(BF16)
            a_ref[2 * chunk + 1] = jnp.where(low, swapped, fill).astype(BF16)
            b_ref[2 * chunk + 1] = jnp.where(low, fill, kk).astype(BF16)

    v = _dot(hn, wv_ref[...])
    vc_ref[...] = v[:, :NSA_KVD].astype(BF16)
    ones = jnp.ones((DEN_ROWS, tm), F32)
    for branch, (e_ref, o_ref, tile) in enumerate(((vse_ref, vso_ref, kt), (vwe_ref, vwo_ref, qt))):
        base = (1 + branch) * NSA_KVD
        for chunk in range(NSA_KVD // LANES):
            v_t = _t_rows(v[:, base + chunk * LANES:base + (chunk + 1) * LANES])
            for par in range(2):
                vg = v_t[par * half:(par + 1) * half]
                even = jnp.concatenate([vg, ones], axis=0).astype(BF16)
                odd = jnp.concatenate([ones, vg], axis=0).astype(BF16)
                for j in range(tm // tile):
                    e_ref[2 * chunk + par, j] = even[:, j * tile:(j + 1) * tile]
                    o_ref[2 * chunk + par, j] = odd[:, j * tile:(j + 1) * tile]
    gate_ref[...] = jax.nn.sigmoid(_dot(hn, wg_ref[...]))


def _nsa_proj(x2d, gain, wqk, wv, wg, c, sa, sb, tm, t, kt, qt):
    m = x2d.shape[0]
    g = NSA_KV_HEADS
    assert SLC_BLOCK == 64 and t % tm == 0 and tm % kt == 0 and tm % qt == 0
    row = lambda i: (i, 0)
    fixed = lambda i: (0, 0)
    key_spec = pl.BlockSpec((g, tm, LANES), lambda i: (0, i, 0))
    key_shape = jax.ShapeDtypeStruct((g, m, LANES), BF16)

    def val(tile):
        return (pl.BlockSpec((g, tm // tile, VAL_ROWS, tile), lambda i: (0, i, 0, 0)),
                jax.ShapeDtypeStruct((g, m // tile, VAL_ROWS, tile), BF16))

    return pl.pallas_call(
        functools.partial(_nsa_proj_kernel, t=t, kt=kt, qt=qt),
        grid=(m // tm,),
        in_specs=[pl.BlockSpec((tm, D_MODEL), row),
                  _resident((1, D_MODEL), fixed),
                  _resident((D_MODEL, NSA_QK), fixed),
                  _resident((D_MODEL, NSA_V), fixed),
                  _resident((D_MODEL, NSA_GATE_PAD), fixed),
                  pl.BlockSpec((tm, LANES), row),
                  pl.BlockSpec((tm, LANES), row),
                  pl.BlockSpec((tm, LANES), row)],
        out_specs=[pl.BlockSpec((tm, NSA_QD), row),
                   pl.BlockSpec((tm, NSA_KVD), row),
                   pl.BlockSpec((tm, NSA_KVD), row),
                   pl.BlockSpec((tm, NSA_GATE_PAD), row),
                   key_spec, key_spec, key_spec, key_spec,
                   val(kt)[0], val(kt)[0], val(qt)[0], val(qt)[0]],
        out_shape=[jax.ShapeDtypeStruct((m, NSA_QD), BF16),
                   jax.ShapeDtypeStruct((m, NSA_KVD), BF16),
                   jax.ShapeDtypeStruct((m, NSA_KVD), BF16),
                   jax.ShapeDtypeStruct((m, NSA_GATE_PAD), F32),
                   key_shape, key_shape, key_shape, key_shape,
                   val(kt)[1], val(kt)[1], val(qt)[1], val(qt)[1]],
        compiler_params=pltpu.CompilerParams(dimension_semantics=("arbitrary",),
                                             vmem_limit_bytes=VMEM_LIMIT),
        name="nsa_proj",
    )(x2d, gain, wqk, wv, wg, c, sa, sb)


def _compress_kernel(a_ref, pos_ref, w1_ref, w2_ref, o_ref):
    a = a_ref[0, 0, 0]
    nc = a.shape[0]
    half = CMP_STRIDE * NSA_HEAD_DIM
    upper = _dot(a, w1_ref[0, :half, :])
    lower = _dot(a, w1_ref[0, half:, :])
    cpos = _dot(pos_ref[0], w1_ref[0])
    hid = upper + pltpu.roll(lower, nc - 1, 0) + cpos[0:1, :]
    hid = hid * jax.nn.sigmoid(hid)
    o_ref[0, 0, 0] = _dot(hid.astype(BF16), w2_ref[0]).astype(BF16)


def _compress(a2, pos8, w1, w2):
    _, b, g, nc, width = a2.shape
    return pl.pallas_call(
        _compress_kernel,
        grid=(2, b, g),
        in_specs=[pl.BlockSpec((1, 1, 1, nc, width), lambda s, bb, gg: (s, bb, gg, 0, 0)),
                  pl.BlockSpec((1, 8, width * 2), lambda s, bb, gg: (s, 0, 0)),
                  pl.BlockSpec((1, width * 2, CMP_HIDDEN), lambda s, bb, gg: (s, 0, 0)),
                  pl.BlockSpec((1, CMP_HIDDEN, NSA_HEAD_DIM), lambda s, bb, gg: (s, 0, 0))],
        out_specs=pl.BlockSpec((1, 1, 1, nc, NSA_HEAD_DIM), lambda s, bb, gg: (s, bb, gg, 0, 0)),
        out_shape=jax.ShapeDtypeStruct((2, b, g, nc, NSA_HEAD_DIM), BF16),
        compiler_params=pltpu.CompilerParams(dimension_semantics=("arbitrary",) * 3),
        name="nsa_compress",
    )(a2, pos8, w1, w2)


def _t_cols(x):
    n = x.shape[1] // LANES
    return jnp.concatenate([x[:, c * LANES:(c + 1) * LANES].T for c in range(n)], axis=0)


def _flash_update(par, st, v_aug, m_sc, acc_sc):
    m_old = m_sc[par]
    m_new = jnp.maximum(m_old, jnp.max(st, axis=0, keepdims=True))
    p = jnp.exp2(st - m_new)
    acc_sc[par] = acc_sc[par] * jnp.exp2(m_old - m_new) + _dot(v_aug, p.astype(BF16))
    m_sc[par] = m_new


def _normalised(acc, par):
    if par == 0:
        return acc[:NSA_HEAD_DIM] * (1.0 / acc[NSA_HEAD_DIM:NSA_HEAD_DIM + 1])
    return acc[DEN_ROWS:] * (1.0 / acc[0:1])


def _nsa_attn_kernel(q_ref, gate_ref, ovl_ref, kca_ref, kcb_ref, vc_ref,
                     ksa_ref, ksb_ref, vse_ref, vso_ref, kwa_ref, kwb_ref, vwe_ref, vwo_ref,
                     *rest, qi, qt, kt, n_sel, n_back):
    o_ref, qm_sc, qaug_sc, x_sc, tot_sc, m_sc, acc_sc, sta_sc, stb_sc = rest[-9:]
    t0 = qi * qt
    half = LANES // 2
    wide = 2 * qt
    lane = lax.broadcasted_iota(jnp.int32, (qt, LANES), 1)
    low = lane < NSA_HEAD_DIM
    zero = jnp.zeros((qt, LANES), BF16)
    qps = [q_ref[0, :, pair * LANES:(pair + 1) * LANES] for pair in range(2)]
    for pair in range(2):
        rows = slice(pair * qt, (pair + 1) * qt)
        qm_sc[0, rows] = jnp.where(low, qps[pair], zero)
        qm_sc[1, rows] = jnp.where(low, zero, qps[pair])

    gates_t = _t_rows(gate_ref[0])

    def gate_row(branch, par):
        return jnp.concatenate([gates_t[branch * 4 + 2 * pair + par:branch * 4 + 2 * pair + par + 1, :]
                                for pair in range(2)], axis=1)

    col = lax.broadcasted_iota(jnp.int32, (1, wide), 1)
    tcol = t0 + jnp.where(col >= qt, col - qt, col)

    t_last = t0 + qt - 1
    nc = min(kca_ref.shape[2], -(-((t_last - CMP_BLOCK + 1) // CMP_STRIDE + 1) // LANES) * LANES)
    kidx = lax.broadcasted_iota(jnp.int32, (nc, wide), 0)
    valid = kidx * CMP_STRIDE + (CMP_BLOCK - 1) <= tcol
    vct = vc_ref[0, 0, :, :nc]
    has_block = tcol >= CMP_BLOCK - 1
    psum = None
    for par in range(2):
        st = _nt_dot((kcb_ref if par else kca_ref)[0, 0, :nc, :], qm_sc[par])
        st = jnp.where(valid, st, NEG)
        m = jnp.max(st, axis=0, keepdims=True)
        e = jnp.exp2(st - m)
        l = jnp.sum(e, axis=0, keepdims=True)
        p = e * jnp.where(has_block, 1.0 / l, 0.0)
        ph = p[:, :qt] + p[:, qt:]
        psum = ph if psum is None else psum + ph
        tot_sc[par] = _dot(vct, p.astype(BF16)) * gate_row(0, par)

    ovl = ovl_ref[:, :nc]
    hi = psum.astype(BF16)
    r1 = psum - hi.astype(F32)
    mid = r1.astype(BF16)
    lo = (r1 - mid.astype(F32)).astype(BF16)
    imp = (_dot(ovl, hi) + _dot(ovl, mid) + _dot(ovl, lo))[:half, :]
    nio = lax.broadcasted_iota(jnp.int32, (half, qt), 0)
    cur = lax.shift_right_logical(t0 + lax.broadcasted_iota(jnp.int32, (half, qt), 1), 6)
    forced = (nio == 0) | (nio == cur) | (nio == cur - 1)
    x_sc[...] = jnp.where(nio > cur, NEG, jnp.where(forced, imp + FORCE, imp))

    sub = lax.broadcasted_iota(jnp.int32, (SUBLANES, qt), 0)
    n_live = (t_last // SLC_BLOCK) // SUBLANES + 1
    xs = [x_sc[r * SUBLANES:(r + 1) * SUBLANES, :] for r in range(n_live)]
    cnts = [jnp.zeros((SUBLANES, qt), jnp.int32) for _ in range(n_live)]
    for rm in range(n_live):
        for mm in range(SUBLANES):
            row = xs[rm][mm:mm + 1, :]
            for r in range(n_live):
                if r < rm:
                    beats = row > xs[r]
                elif r > rm:
                    beats = row >= xs[r]
                else:
                    beats = (row > xs[r]) | ((row == xs[r]) & (sub > mm))
                cnts[r] = cnts[r] + jnp.where(beats, 1, 0)
    bias_t = jnp.where(jnp.concatenate(cnts, axis=0) < n_sel, 0.0, NEG).astype(F32)
    if n_live * SUBLANES < half:
        bias_t = jnp.concatenate([bias_t, jnp.full((half - n_live * SUBLANES, qt), NEG, F32)], axis=0)
    bias = _t_cols(jnp.concatenate([bias_t, bias_t], axis=0)).astype(BF16)
    for pair in range(2):
        rows = slice(pair * qt, (pair + 1) * qt)
        qaug_sc[0, rows] = jnp.where(low, qps[pair], bias)
        qaug_sc[1, rows] = jnp.where(low, bias, qps[pair])

    ri = lax.broadcasted_iota(jnp.int32, (qt, wide), 0)
    ci = lax.broadcasted_iota(jnp.int32, (qt, wide), 1)
    ci = jnp.where(ci >= qt, ci - qt, ci)
    backs = [back for back in range(n_back + 1) if qi >= back]
    for par in range(2):
        k_ref = kwb_ref if par else kwa_ref
        v_ref = vwo_ref if par else vwe_ref
        sts = []
        for back in backs:
            st = _nt_dot(k_ref[0, (qi - back) * qt:(qi - back + 1) * qt, :], qm_sc[par])
            if back == 0:
                st = jnp.where(ri <= ci, st, NEG)
            elif back == n_back:
                st = jnp.where(ri > ci, st, NEG)
            sts.append(st)
        m = functools.reduce(jnp.maximum, [jnp.max(st, axis=0, keepdims=True) for st in sts])
        acc = None
        for back, st in zip(backs, sts):
            part = _dot(v_ref[0, qi - back], jnp.exp2(st - m).astype(BF16))
            acc = part if acc is None else acc + part
        tot_sc[par] = tot_sc[par] + _normalised(acc, par) * gate_row(2, par)

    m_sc[...] = jnp.full(m_sc.shape, NEG, F32)
    acc_sc[...] = jnp.zeros_like(acc_sc)

    def scores(kidx, buf):
        ks = kidx * kt if isinstance(kidx, int) else pl.multiple_of(kidx * kt, kt)
        buf[0] = _nt_dot(ksa_ref[0, pl.ds(ks, kt), :], qaug_sc[0])
        buf[1] = _nt_dot(ksb_ref[0, pl.ds(ks, kt), :], qaug_sc[1])

    def consume(kidx, buf, masked):
        v_aug = (vse_ref[0, kidx], vso_ref[0, kidx])
        if masked:
            keep = kidx * kt + lax.broadcasted_iota(jnp.int32, (kt, wide), 0) <= tcol
        sts, m_olds, m_news = [], [], []
        for par in range(2):
            st = buf[par]
            if masked:
                st = jnp.where(keep, st, NEG)
            sts.append(st)
            m_olds.append(m_sc[par])
            m_news.append(jnp.maximum(m_olds[par], jnp.max(st, axis=0, keepdims=True)))
        ps = [jnp.exp2(sts[par] - m_news[par]).astype(BF16) for par in range(2)]
        for par in range(2):
            acc_sc[par] = acc_sc[par] * jnp.exp2(m_olds[par] - m_news[par]) + _dot(v_aug[par], ps[par])
            m_sc[par] = m_news[par]

    assert qt <= kt
    last = t_last // kt
    scores(0, sta_sc)

    def body(j, carry):
        scores(2 * j + 1, stb_sc)
        consume(2 * j, sta_sc, False)
        scores(2 * j + 2, sta_sc)
        consume(2 * j + 1, stb_sc, False)
        return carry

    if last % 2 == 1 and last >= 5:
        lax.fori_loop(0, last // 2, body, 0)
        scores(last, stb_sc)
        consume(last - 1, sta_sc, False)
        consume(last, stb_sc, True)
    else:
        bufs = (sta_sc, stb_sc)
        for kidx in range(last):
            scores(kidx + 1, bufs[(kidx + 1) % 2])
            consume(kidx, bufs[kidx % 2], False)
        consume(last, bufs[last % 2], True)

    total = [tot_sc[par] + _normalised(acc_sc[par], par) * gate_row(1, par) for par in range(2)]
    for pair in range(2):
        cols = slice(pair * qt, (pair + 1) * qt)
        pair_t = jnp.concatenate([total[0][:, cols], total[1][:, cols]], axis=0)
        o_ref[0, :, pair * LANES:(pair + 1) * LANES] = _t_cols(pair_t).astype(BF16)


def _nsa_attn(qk3, gates3, ovl, kca, kcb, vc2, ksa, ksb, vse, vso, kwa, kwb, vwe, vwo, qt, kt):
    b, t, _ = qk3.shape
    g = NSA_KV_HEADS
    nc = kca.shape[2]
    n_blocks = t // SLC_BLOCK
    assert n_blocks <= LANES // 2 and SLC_BLOCK == 64 and WINDOW % qt == 0
    per_group = lambda *tail: (lambda bb, gg: (bb, gg) + tail)
    kc_spec = pl.BlockSpec((1, 1, nc, LANES), per_group(0, 0))
    k_spec = pl.BlockSpec((1, t, LANES), lambda bb, gg: (gg, bb, 0))
    vs_spec = pl.BlockSpec((1, t // kt, VAL_ROWS, kt), lambda bb, gg: (gg, bb, 0, 0))
    vw_spec = pl.BlockSpec((1, t // qt, VAL_ROWS, qt), lambda bb, gg: (gg, bb, 0, 0))
    half = LANES // 2
    args = [qk3, gates3, ovl, kca, kcb, vc2, ksa, ksb, vse, vso, kwa, kwb, vwe, vwo]
    out = None
    for qi in range(t // qt):
        tile_map = lambda bb, gg, qi=qi: (bb, qi, gg)
        in_specs = [pl.BlockSpec((1, qt, MXU_DIM), tile_map),
                    pl.BlockSpec((1, qt, LANES), tile_map),
                    pl.BlockSpec((LANES, nc), lambda bb, gg: (0, 0)),
                    kc_spec, kc_spec,
                    pl.BlockSpec((1, 1, half, nc), per_group(0, 0)),
                    k_spec, k_spec, vs_spec, vs_spec,
                    k_spec, k_spec, vw_spec, vw_spec]
        if out is not None:
            in_specs.append(pl.BlockSpec(memory_space=pl.ANY))
        out = pl.pallas_call(
            functools.partial(_nsa_attn_kernel, qi=qi, qt=qt, kt=kt,
                              n_sel=min(SLC_TOPK, n_blocks), n_back=WINDOW // qt),
            grid=(b, g),
            in_specs=in_specs,
            out_specs=pl.BlockSpec((1, qt, MXU_DIM), tile_map),
            out_shape=jax.ShapeDtypeStruct((b, t, NSA_QD), BF16),
            scratch_shapes=[pltpu.VMEM((2, 2 * qt, LANES), BF16),
                            pltpu.VMEM((2, 2 * qt, LANES), BF16),
                            pltpu.VMEM((half, qt), F32),
                            pltpu.VMEM((2, half, 2 * qt), F32),
                            pltpu.VMEM((2, 1, 2 * qt), F32),
                            pltpu.VMEM((2, VAL_ROWS, 2 * qt), F32),
                            pltpu.VMEM((2, kt, 2 * qt), F32),
                            pltpu.VMEM((2, kt, 2 * qt), F32)],
            input_output_aliases={} if out is None else {len(args): 0},
            compiler_params=pltpu.CompilerParams(dimension_semantics=("arbitrary",) * 2,
                                                 vmem_limit_bytes=VMEM_LIMIT),
            name=f"nsa_attn_q{qi}",
        )(*(args if out is None else args + [out]))
    return out


def _ret_rope_consts():
    half = RET_QK_DIM // 2
    inv_freq = RET_ROPE_THETA ** (-2.0 * jnp.arange(half, dtype=F32) / RET_QK_DIM)
    rows = jnp.zeros((8, LANES), F32)
    return rows.at[0].set(inv_freq).at[1].set(1.0)


def _nsa_rope_consts():
    half = NSA_ROPE_DIM // 2
    inv_freq = NSA_ROPE_THETA ** (-2.0 * jnp.arange(half, dtype=F32) / NSA_ROPE_DIM)
    r = np.arange(LANES) % NSA_HEAD_DIM
    rot = r < NSA_ROPE_DIM
    freq = jnp.where(rot, inv_freq[r % half], 0.0)
    mask_a = ((r >= half) & rot).astype(np.float32)
    mask_b = -(r < half).astype(np.float32)
    rows = jnp.zeros((8, LANES), F32)
    return rows.at[0].set(freq).at[1].set(mask_a).at[2].set(mask_b)


def _nsa_weight_columns():
    kv0 = NSA_QD
    cols = list(range(NSA_QD))
    for kvsel in range(2):
        for branch in range(3):
            base = kv0 + (branch * 2 + kvsel) * NSA_KVD
            cols += list(range(base, base + NSA_KVD))
    gate0 = NSA_QD + 6 * NSA_KVD
    hg = NSA_HEADS // NSA_KV_HEADS
    for g in range(NSA_KV_HEADS):
        lanes = [-1] * LANES
        for branch in range(3):
            for hh in range(hg):
                lanes[branch * 4 + hh] = gate0 + (g * hg + hh) * 3 + branch
        cols += lanes
    return np.asarray(cols, np.int32)


def _group_major(a, b, t):
    return a.reshape(b, t, NSA_KV_HEADS, NSA_HEAD_DIM).transpose(0, 2, 1, 3)


def _pad_pair(k):
    z = jnp.zeros_like(k)
    return jnp.concatenate([k, z], axis=-1), jnp.concatenate([z, k], axis=-1)


def _retention_layer(x2d, b, t, gain, w_in, w_out, cos, sin, tm):
    qk, v, gate = _ret_proj(x2d, gain.reshape(1, D_MODEL), w_in.astype(BF16), cos, sin, tm)
    wide = lambda a: a.reshape(b, t, 2 * D_MODEL)
    o = _retention(wide(qk), wide(v), wide(gate), min(t, 1024))
    return _matmul_res(o.reshape(b * t, RET_HEADS * RET_V_DIM), w_out.astype(BF16), x2d, tm)


def _nsa_layer(x2d, b, t, gain, w_in, cmp_pos, cmp_w1, cmp_w2, w_out, tabs, tm):
    g = NSA_KV_HEADS
    cols = _nsa_weight_columns()
    pick = lambda lo, hi: jnp.concatenate(
        [w_in[:, c0:c0 + NSA_KVD] for c0 in cols[lo:hi:NSA_KVD]], axis=1).astype(BF16)
    gate_cols = cols[NSA_QK + NSA_V:]
    w_ext = jnp.concatenate([w_in, jnp.zeros((D_MODEL, 1), w_in.dtype)], axis=1)
    w_gate = jnp.take(w_ext, jnp.asarray(np.where(gate_cols < 0, w_in.shape[1], gate_cols)),
                      axis=1).astype(BF16)
    qt = min(t, 512)
    kt = min(t, 512)
    q, k_cmp_tok, v_cmp_tok, gates, *slc_win = _nsa_proj(
        x2d, gain.reshape(1, D_MODEL), pick(0, NSA_QK), pick(NSA_QK, NSA_QK + NSA_V), w_gate,
        *tabs, tm, t, kt, qt)
    ksa, ksb, kwa, kwb, vse, vso, vwe, vwo = slc_win

    nc = t // CMP_STRIDE
    a2 = jnp.stack([_group_major(k_cmp_tok, b, t), _group_major(v_cmp_tok, b, t)])
    a2 = a2.reshape(2, b, g, nc, CMP_STRIDE * NSA_HEAD_DIM)
    pos8 = jnp.broadcast_to(cmp_pos.reshape(2, 1, CMP_BLOCK * NSA_HEAD_DIM),
                            (2, 8, CMP_BLOCK * NSA_HEAD_DIM)).astype(BF16)
    cmp = _compress(a2, pos8, cmp_w1.astype(BF16), cmp_w2.astype(BF16))
    kca, kcb = _pad_pair(cmp[0])
    vc2 = cmp[1].transpose(0, 1, 3, 2)

    n_blocks = t // SLC_BLOCK
    kk = np.arange(nc)[None, :]
    nn = np.arange(LANES)[:, None]
    ovl = ((kk * CMP_STRIDE < nn * SLC_BLOCK + SLC_BLOCK) & (kk * CMP_STRIDE + CMP_BLOCK > nn * SLC_BLOCK)
           & (nn < n_blocks) & (kk < nc - 1))
    ovl = jnp.asarray(ovl, BF16)

    attn = _nsa_attn(q.reshape(b, t, NSA_QD), gates.reshape(b, t, NSA_GATE_PAD), ovl, kca, kcb, vc2,
                     ksa, ksb, vse, vso, kwa, kwb, vwe, vwo, qt, kt)
    return _matmul_res(attn.reshape(b * t, NSA_QD), w_out.astype(BF16), x2d, tm)


def _ffn_layer(x2d, gain, w_gu, w_down, tm, final_gain=None):
    fg = None if final_gain is None else final_gain.reshape(1, D_MODEL)
    return _ffn(x2d, gain.reshape(1, D_MODEL), w_gu.astype(BF16), w_down.astype(BF16), tm, final_gain=fg)


def kernel(x, positions, norm_mix, norm_ffn, norm_final, ret_w_in, ret_w_out, nsa_w_in, nsa_cmp_pos,
           nsa_cmp_w1, nsa_cmp_w2, nsa_w_out, ffn_w_gu, ffn_w_down):
    b, t, _ = x.shape
    m = b * t
    tm = min(m, ROW_TILE)
    x2d = x.reshape(m, D_MODEL)
    pos_col = positions.reshape(m, 1).astype(jnp.int32)
    ret_cos, ret_sin, _ = _rope_tables(pos_col, _ret_rope_consts(), tm)
    nsa_tabs = _rope_tables(pos_col, _nsa_rope_consts(), tm)
    for i in range(DEPTH):
        j = i // 2
        if i % 2 == 0:
            x2d = _retention_layer(x2d, b, t, norm_mix[i], ret_w_in[j], ret_w_out[j], ret_cos, ret_sin, tm)
        else:
            x2d = _nsa_layer(x2d, b, t, norm_mix[i], nsa_w_in[j], nsa_cmp_pos[j], nsa_cmp_w1[j],
                             nsa_cmp_w2[j], nsa_w_out[j], nsa_tabs, tm)
        x2d = _ffn_layer(x2d, norm_ffn[i], ffn_w_gu[i], ffn_w_down[i], tm,
                         final_gain=norm_final if i == DEPTH - 1 else None)
    return x2d.reshape(b, t, D_MODEL)
```
